```python
import jax, jax.numpy as jnp
from jax import lax
import numpy as np

D_MODEL = 2048
BATCH = 4
SEQ = 4096
DEPTH = 2

EPS = 1e-6
MLA_HEADS = 8
Q_LORA_RANK = 512
KV_LORA_RANK = 512
QK_NOPE_DIM = 128
QK_ROPE_DIM = 64
QK_HEAD_DIM = QK_NOPE_DIM + QK_ROPE_DIM
V_HEAD_DIM = 128
ROPE_THETA = 10000.0
Q_BLOCK = 128
MLA_KV_IN = KV_LORA_RANK + QK_ROPE_DIM
CONV_GROUPS = 8
CONV_GROUP_WIDTH = 64
CONV_CHANNELS = CONV_GROUPS * CONV_GROUP_WIDTH
CONV_WIDTH = 31
GDN_HEADS = 4
GDN_K_DIM = 128
GDN_V_DIM = 128
GDN_SHORT_CONV = 5
GDN_CHUNK = 64
GDN_QK_W = GDN_HEADS * GDN_K_DIM
GDN_V_W = GDN_HEADS * GDN_V_DIM
D_IN = Q_LORA_RANK + MLA_KV_IN + 2 * CONV_CHANNELS + 2 * GDN_QK_W + 2 * GDN_V_W + 4 * GDN_HEADS
D_MIX = MLA_HEADS * V_HEAD_DIM + CONV_CHANNELS + GDN_V_W
D_FF = -(-8 * D_MODEL // (3 * 256)) * 256

kernel_name = "hybrid_mla_conformer_gdn_encoder"


def rmsnorm(x, w):
    xf = x.astype(jnp.float32)
    y = xf * lax.rsqrt(jnp.mean(xf * xf, axis=-1, keepdims=True) + EPS)
    return (y * w.astype(jnp.float32)).astype(x.dtype)


def layernorm(x, g, b):
    xf = x.astype(jnp.float32)
    mu = jnp.mean(xf, axis=-1, keepdims=True)
    var = jnp.mean(jnp.square(xf - mu), axis=-1, keepdims=True)
    y = (xf - mu) * lax.rsqrt(var + EPS)
    return (y * g.astype(jnp.float32) + b.astype(jnp.float32)).astype(x.dtype)


def l2norm(x):
    return x * lax.rsqrt(jnp.sum(x * x, axis=-1, keepdims=True) + EPS)


def rope_tables(positions):
    inv = 1.0 / (ROPE_THETA ** (jnp.arange(0, QK_ROPE_DIM, 2, dtype=jnp.float32) / QK_ROPE_DIM))
    ang = positions.astype(jnp.float32)[..., None] * inv
    return jnp.cos(ang), jnp.sin(ang)


def apply_rope(t, cos, sin):
    half = t.shape[-1] // 2
    tf = t.astype(jnp.float32)
    t1, t2 = tf[..., :half], tf[..., half:]
    return jnp.concatenate([t1 * cos - t2 * sin, t2 * cos + t1 * sin], axis=-1).astype(t.dtype)


def depthwise_conv(h, w):
    pad = w.shape[0] // 2
    return lax.conv_general_dilated(h, w.astype(h.dtype), window_strides=(1,), padding=[(pad, pad)],
                                    dimension_numbers=('NWC', 'WIO', 'NWC'),
                                    feature_group_count=h.shape[-1])


def mla_mixer(c_q, c_kv_rope, q_a_norm, w_uq, kv_a_norm, w_ukv, cos, sin):
    B, S, _ = c_q.shape
    q = (rmsnorm(c_q, q_a_norm) @ w_uq).reshape(B, S, MLA_HEADS, QK_HEAD_DIM)
    q_nope = q[..., :QK_NOPE_DIM]
    q_rope = apply_rope(q[..., QK_NOPE_DIM:], cos[:, :, None, :], sin[:, :, None, :])
    c_kv = c_kv_rope[..., :KV_LORA_RANK]
    k_rope = apply_rope(c_kv_rope[..., KV_LORA_RANK:], cos, sin)
    kv = (rmsnorm(c_kv, kv_a_norm) @ w_ukv).reshape(B, S, MLA_HEADS, QK_NOPE_DIM + V_HEAD_DIM)
    k_nope, v = kv[..., :QK_NOPE_DIM], kv[..., QK_NOPE_DIM:]
    scale = QK_HEAD_DIM ** -0.5
    nb = S // Q_BLOCK

    def blocks(t):
        return jnp.moveaxis(t.reshape(B, nb, Q_BLOCK, *t.shape[2:]), 1, 0)

    def attend(qs):
        qn, qr = qs
        s = (jnp.einsum('bqhd,bkhd->bhqk', qn, k_nope, preferred_element_type=jnp.float32)
             + jnp.einsum('bqhr,bkr->bhqk', qr, k_rope, preferred_element_type=jnp.float32)) * scale
        p = jax.nn.softmax(s, axis=-1).astype(v.dtype)
        return jnp.einsum('bhqk,bkhd->bqhd', p, v)

    o = lax.map(attend, (blocks(q_nope), blocks(q_rope)))
    return jnp.moveaxis(o, 0, 1).reshape(B, S, MLA_HEADS * V_HEAD_DIM)


def conformer_conv(u, dw_w, dw_b, ln_g, ln_b):
    a, gate = jnp.split(u, 2, axis=-1)
    h = a * jax.nn.sigmoid(gate)
    h = depthwise_conv(h, dw_w) + dw_b.astype(h.dtype)
    h = layernorm(h, ln_g, ln_b)
    return jax.nn.silu(h)


def chunk_gated_delta(q, k, v, g, beta):
    B, S, H, K = q.shape
    V = v.shape[-1]
    C = GDN_CHUNK
    N = S // C

    def chunks(t):
        t = t.reshape(B, N, C, H, *t.shape[3:])
        return jnp.moveaxis(jnp.moveaxis(t, 1, 0), 2, 3)

    q, k, v, g, beta = chunks(q), chunks(k), chunks(v), chunks(g), chunks(beta)
    gc = jnp.cumsum(g, axis=-1)
    kb = k * beta[..., None]
    vb = v * beta[..., None]
    idx = jnp.arange(C)
    incl = idx[:, None] >= idx[None, :]
    strict = idx[:, None] > idx[None, :]
    diff = gc[..., :, None] - gc[..., None, :]
    decay = jnp.where(incl, jnp.exp(jnp.where(incl, diff, 0.0)), 0.0)
    a_mat = jnp.where(strict, jnp.einsum('nbhik,nbhjk->nbhij', kb, k) * decay, 0.0)
    eye = jnp.eye(C, dtype=jnp.float32)
    t_inv = lax.linalg.triangular_solve(eye + a_mat, jnp.broadcast_to(eye, a_mat.shape),
                                        left_side=True, lower=True, unit_diagonal=True)
    u = jnp.einsum('nbhij,nbhjv->nbhiv', t_inv, vb)
    w = jnp.einsum('nbhij,nbhjk->nbhik', t_inv, kb * jnp.exp(gc)[..., None])
    qk = jnp.einsum('nbhik,nbhjk->nbhij', q, k) * decay

    def step(state, xs):
        q_c, k_c, u_c, w_c, qk_c, g_c = xs
        v_new = u_c - jnp.einsum('bhck,bhkv->bhcv', w_c, state)
        o_c = (jnp.einsum('bhck,bhkv->bhcv', q_c * jnp.exp(g_c)[..., None], state)
               + jnp.einsum('bhij,bhjv->bhiv', qk_c, v_new))
        g_last = g_c[..., -1:]
        state = (state * jnp.exp(g_last)[..., None]
                 + jnp.einsum('bhck,bhcv->bhkv', k_c * jnp.exp(g_last - g_c)[..., None], v_new))
        return state, o_c

    state0 = jnp.zeros((B, H, K, V), jnp.float32)
    _, o = lax.scan(step, state0, (q, k, u, w, qk, gc))
    return jnp.moveaxis(jnp.moveaxis(o, 3, 2), 0, 1).reshape(B, S, H, V)


def gdn_mixer(q, k, v, z, a_f, b_f, a_b, b_b, conv_w, a_log, dt_bias, out_norm):
    B, S, _ = q.shape
    f32 = jnp.float32
    qkv = jax.nn.silu(depthwise_conv(jnp.concatenate([q, k, v], axis=-1), conv_w))
    q, k, v = jnp.split(qkv, [GDN_QK_W, 2 * GDN_QK_W], axis=-1)
    q = l2norm(q.reshape(B, S, GDN_HEADS, GDN_K_DIM).astype(f32)) * (GDN_K_DIM ** -0.5)
    k = l2norm(k.reshape(B, S, GDN_HEADS, GDN_K_DIM).astype(f32))
    v = v.reshape(B, S, GDN_HEADS, GDN_V_DIM).astype(f32)

    def gates(a, b, a_log_d, dt_bias_d):
        g = -jnp.exp(a_log_d.astype(f32)) * jax.nn.softplus(a.astype(f32) + dt_bias_d.astype(f32))
        return g, jax.nn.sigmoid(b.astype(f32))

    g_f, beta_f = gates(a_f, b_f, a_log[0], dt_bias[0])
    g_b, beta_b = gates(a_b, b_b, a_log[1], dt_bias[1])
    o_f = chunk_gated_delta(q, k, v, g_f, beta_f)
    o_b = jnp.flip(chunk_gated_delta(jnp.flip(q, 1), jnp.flip(k, 1), jnp.flip(v, 1),
                                     jnp.flip(g_b, 1), jnp.flip(beta_b, 1)), 1)
    o = rmsnorm(o_f + o_b, out_norm) * jax.nn.silu(z.reshape(B, S, GDN_HEADS, GDN_V_DIM).astype(f32))
    return o.reshape(B, S, GDN_V_W).astype(z.dtype)


def hybrid_layer(x, cos, sin, pre_mix_norm, w_in, q_a_norm, w_uq, kv_a_norm, w_ukv,
                 conv_dw_w, conv_dw_b, conv_ln_g, conv_ln_b, gdn_conv_w, gdn_a_log, gdn_dt_bias,
                 gdn_out_norm, w_out, post_mix_norm, pre_ffn_norm, w_gate, w_up, w_down, post_ffn_norm):
    h = rmsnorm(x, pre_mix_norm)
    proj = h @ w_in
    sizes = (Q_LORA_RANK, MLA_KV_IN, 2 * CONV_CHANNELS, GDN_QK_W, GDN_QK_W, GDN_V_W, GDN_V_W,
             GDN_HEADS, GDN_HEADS, GDN_HEADS, GDN_HEADS)
    c_q, c_kv, conv_in, g_q, g_k, g_v, g_z, a_f, b_f, a_b, b_b = jnp.split(
        proj, np.cumsum(sizes)[:-1].tolist(), axis=-1)
    o_a = mla_mixer(c_q, c_kv, q_a_norm, w_uq, kv_a_norm, w_ukv, cos, sin)
    o_b = conformer_conv(conv_in, conv_dw_w, conv_dw_b, conv_ln_g, conv_ln_b)
    o_c = gdn_mixer(g_q, g_k, g_v, g_z, a_f, b_f, a_b, b_b, gdn_conv_w, gdn_a_log, gdn_dt_bias, gdn_out_norm)
    mix = jnp.concatenate([o_a, o_b.astype(o_a.dtype), o_c.astype(o_a.dtype)], axis=-1) @ w_out
    x = x + rmsnorm(mix, post_mix_norm)
    h = rmsnorm(x, pre_ffn_norm)
    f = (jax.nn.silu(h @ w_gate) * (h @ w_up)) @ w_down
    return x + rmsnorm(f, post_ffn_norm)


def setup_inputs(seed: int = 0) -> dict:
    key = jax.random.key(seed)
    ks = iter(jax.random.split(key, 32))
    L = DEPTH
    f32 = jnp.float32

    def nrm(shape, scale):
        return jax.random.normal(next(ks), shape, f32) * scale

    def gain(n):
        return 1.0 + 0.01 * jax.random.normal(next(ks), (L, n), f32)

    x = jax.random.normal(next(ks), (BATCH, SEQ, D_MODEL), f32)
    positions = (jnp.arange(SEQ, dtype=jnp.int32)[None, :]
                 + jax.random.randint(next(ks), (BATCH, 1), 0, 1024, dtype=jnp.int32))
    pre_mix_norm = gain(D_MODEL)
    w_in = nrm((L, D_MODEL, D_IN), D_MODEL ** -0.5)
    q_a_norm = gain(Q_LORA_RANK)
    w_uq = nrm((L, Q_LORA_RANK, MLA_HEADS * QK_HEAD_DIM), Q_LORA_RANK ** -0.5)
    kv_a_norm = gain(KV_LORA_RANK)
    w_ukv = nrm((L, KV_LORA_RANK, MLA_HEADS * (QK_NOPE_DIM + V_HEAD_DIM)), KV_LORA_RANK ** -0.5)
    conv_dw_w = nrm((L, CONV_WIDTH, 1, CONV_CHANNELS), CONV_WIDTH ** -0.5)
    conv_dw_b = nrm((L, CONV_CHANNELS), 0.01)
    conv_ln_g = gain(CONV_CHANNELS)
    conv_ln_b = nrm((L, CONV_CHANNELS), 0.01)
    gdn_conv_w = nrm((L, GDN_SHORT_CONV, 1, 2 * GDN_QK_W + GDN_V_W), GDN_SHORT_CONV ** -0.5)
    gdn_a_log = jnp.log(jax.random.uniform(next(ks), (L, 2, GDN_HEADS), f32, minval=1.0, maxval=16.0))
    dt = jnp.exp(jax.random.uniform(next(ks), (L, 2, GDN_HEADS), f32,
                                    minval=float(np.log(1e-3)), maxval=float(np.log(1e-1))))
    gdn_dt_bias = dt + jnp.log(-jnp.expm1(-dt))
    gdn_out_norm = gain(GDN_V_DIM)
    w_out = nrm((L, D_MIX, D_MODEL), D_MIX ** -0.5)
    post_mix_norm = gain(D_MODEL)
    pre_ffn_norm = gain(D_MODEL)
    w_gate = nrm((L, D_MODEL, D_FF), D_MODEL ** -0.5)
    w_up = nrm((L, D_MODEL, D_FF), D_MODEL ** -0.5)
    w_down = nrm((L, D_FF, D_MODEL), D_FF ** -0.5)
    post_ffn_norm = gain(D_MODEL)
    return {"x": x, "positions": positions, "pre_mix_norm": pre_mix_norm, "w_in": w_in,
            "q_a_norm": q_a_norm, "w_uq": w_uq, "kv_a_norm": kv_a_norm, "w_ukv": w_ukv,
            "conv_dw_w": conv_dw_w, "conv_dw_b": conv_dw_b, "conv_ln_g": conv_ln_g,
            "conv_ln_b": conv_ln_b, "gdn_conv_w": gdn_conv_w, "gdn_a_log": gdn_a_log,
            "gdn_dt_bias": gdn_dt_bias, "gdn_out_norm": gdn_out_norm, "w_out": w_out,
            "post_mix_norm": post_mix_norm, "pre_ffn_norm": pre_ffn_norm, "w_gate": w_gate,
            "w_up": w_up, "w_down": w_down, "post_ffn_norm": post_ffn_norm}


def reference(x, positions, pre_mix_norm, w_in, q_a_norm, w_uq, kv_a_norm, w_ukv, conv_dw_w,
              conv_dw_b, conv_ln_g, conv_ln_b, gdn_conv_w, gdn_a_log, gdn_dt_bias, gdn_out_norm,
              w_out, post_mix_norm, pre_ffn_norm, w_gate, w_up, w_down, post_ffn_norm):
    cos, sin = rope_tables(positions)
    for l in range(DEPTH):
        x = hybrid_layer(x, cos, sin, pre_mix_norm[l], w_in[l], q_a_norm[l], w_uq[l], kv_a_norm[l],
                         w_ukv[l], conv_dw_w[l], conv_dw_b[l], conv_ln_g[l], conv_ln_b[l],
                         gdn_conv_w[l], gdn_a_log[l], gdn_dt_bias[l], gdn_out_norm[l], w_out[l],
                         post_mix_norm[l], pre_ffn_norm[l], w_gate[l], w_up[l], w_down[l],
                         post_ffn_norm[l])
    return x
```

```python
import functools

import numpy as np
import jax
import jax.numpy as jnp
from jax import lax
from jax.experimental import pallas as pl
from jax.experimental.pallas import tpu as pltpu

F32 = jnp.float32
BF16 = jnp.bfloat16

EPS = 1e-6
LANES = 128
VMEM_LIMIT = 56 * 1024 * 1024

MLA_HEADS = 8
Q_LORA_RANK = 512
KV_LORA_RANK = 512
QK_NOPE_DIM = 128
QK_ROPE_DIM = 64
QK_HEAD_DIM = QK_NOPE_DIM + QK_ROPE_DIM
V_HEAD_DIM = 128
ROPE_THETA = 10000.0
QK_PAD_DIM = 2 * LANES
CONV_CHANNELS = 512
CONV_WIDTH = 31
GDN_HEADS = 4
GDN_K_DIM = 128
GDN_V_DIM = 128
GDN_SHORT_CONV = 5
GDN_W = GDN_HEADS * GDN_K_DIM
GDN_CHUNK = 256
GATE_LANE0 = QK_ROPE_DIM


def _params(semantics):
    return pltpu.CompilerParams(dimension_semantics=semantics, vmem_limit_bytes=VMEM_LIMIT)


def _resident(shape):
    return pl.BlockSpec(shape, lambda *_: (0,) * len(shape), pipeline_mode=pl.Buffered(1))


def _rms(x, w):
    return x * lax.rsqrt(jnp.mean(x * x, axis=-1, keepdims=True) + EPS) * w


def _sigmoid(x):
    return 1.0 / (1.0 + jnp.exp(-x))


def _silu(x):
    return x * _sigmoid(x)


def _rope_table_kernel(pos_ref, inv_ref, cos_ref, sin_ref):
    ang = pos_ref[...].astype(F32) * inv_ref[...]
    lane = lax.broadcasted_iota(jnp.int32, ang.shape, 1)
    valid = lane < QK_ROPE_DIM
    cos_ref[...] = jnp.where(valid, jnp.cos(ang), 0.0)
    sin_ref[...] = jnp.where(valid, jnp.sin(ang), 0.0)


def _rope_tables(positions):
    t = positions.size
    tm = min(t, 2048)
    half = QK_ROPE_DIM // 2
    inv = 1.0 / (ROPE_THETA ** (jnp.arange(0, QK_ROPE_DIM, 2, dtype=F32) / QK_ROPE_DIM))
    inv_row = jnp.zeros((1, LANES), F32).at[0, :half].set(inv).at[0, half:2 * half].set(inv)
    spec = pl.BlockSpec((tm, LANES), lambda i: (i, 0))
    return pl.pallas_call(
        _rope_table_kernel,
        grid=(t // tm,),
        in_specs=[pl.BlockSpec((tm, 1), lambda i: (i, 0)), _resident((1, LANES))],
        out_specs=[spec, spec],
        out_shape=[jax.ShapeDtypeStruct((t, LANES), F32)] * 2,
        compiler_params=_params(("parallel",)),
        name="rope_tables",
    )(positions.reshape(t, 1), inv_row)


def _in_proj_kernel(x_ref, g_ref, w_ref, *out_refs):
    xn = _rms(x_ref[...], g_ref[...]).astype(BF16)
    off = 0
    for o_ref in out_refs:
        n = o_ref.shape[1]
        o_ref[...] = jnp.dot(xn, w_ref[:, off:off + n], preferred_element_type=F32).astype(o_ref.dtype)
        off += n


def _in_proj(x, gain, w, widths, tm=512):
    t, d = x.shape
    n = w.shape[1]
    assert sum(widths) == n
    return pl.pallas_call(
        _in_proj_kernel,
        grid=(t // tm,),
        in_specs=[pl.BlockSpec((tm, d), lambda i: (i, 0)), _resident((1, d)), _resident((d, n))],
        out_specs=[pl.BlockSpec((tm, wd), lambda i: (i, 0)) for wd in widths],
        out_shape=[jax.ShapeDtypeStruct((t, wd), F32) for wd in widths],
        compiler_params=_params(("parallel",)),
        name="in_proj",
    )(x, gain.reshape(1, d), w)


def _mla_prep_kernel(cq_ref, ckv_ref, misc_ref, krot_ref, cos_ref, sin_ref, qn_ref, kvn_ref,
                     wq_ref, wkv_ref, q_out, k_out, v_out, *, q_scale):
    cos = cos_ref[...]
    sin = sin_ref[...]
    hw = MLA_HEADS * LANES
    cqn = _rms(cq_ref[...], qn_ref[...]).astype(BF16)
    q_nope = jnp.dot(cqn, wq_ref[:, :hw], preferred_element_type=F32)
    q_rope = jnp.dot(cqn, wq_ref[:, hw:2 * hw], preferred_element_type=F32)
    q_rot = jnp.dot(cqn, wq_ref[:, 2 * hw:], preferred_element_type=F32)
    ckvn = _rms(ckv_ref[...], kvn_ref[...]).astype(BF16)
    k_nope = jnp.dot(ckvn, wkv_ref[:, :hw], preferred_element_type=F32)
    v_out[...] = jnp.dot(ckvn, wkv_ref[:, hw:], preferred_element_type=F32).astype(v_out.dtype)
    k_rope = (misc_ref[...] * cos + krot_ref[...] * sin).astype(k_out.dtype)
    for h in range(MLA_HEADS):
        blk = slice(h * LANES, (h + 1) * LANES)
        lo = slice(h * QK_PAD_DIM, h * QK_PAD_DIM + LANES)
        hi = slice(h * QK_PAD_DIM + LANES, (h + 1) * QK_PAD_DIM)
        q_out[:, lo] = (q_nope[:, blk] * q_scale).astype(q_out.dtype)
        q_out[:, hi] = ((q_rope[:, blk] * cos + q_rot[:, blk] * sin) * q_scale).astype(q_out.dtype)
        k_out[:, lo] = k_nope[:, blk].astype(k_out.dtype)
        k_out[:, hi] = k_rope


def _mla_prep(cq, ckv, misc, krot, cos, sin, q_a_norm, kv_a_norm, wq, wkv, q_scale, tm=512):
    t = cq.shape[0]
    row = lambda n: pl.BlockSpec((tm, n), lambda i: (i, 0))
    qk_w = MLA_HEADS * QK_PAD_DIM
    v_w = MLA_HEADS * V_HEAD_DIM
    return pl.pallas_call(
        functools.partial(_mla_prep_kernel, q_scale=q_scale),
        grid=(t // tm,),
        in_specs=[row(Q_LORA_RANK), row(KV_LORA_RANK), row(LANES), row(LANES), row(LANES), row(LANES),
                  _resident((1, Q_LORA_RANK)), _resident((1, KV_LORA_RANK)),
                  _resident(wq.shape), _resident(wkv.shape)],
        out_specs=[row(qk_w), row(qk_w), row(v_w)],
        out_shape=[jax.ShapeDtypeStruct((t, qk_w), BF16), jax.ShapeDtypeStruct((t, qk_w), BF16),
                   jax.ShapeDtypeStruct((t, v_w), BF16)],
        compiler_params=_params(("parallel",)),
        name="mla_prep",
    )(cq, ckv, misc, krot, cos, sin, q_a_norm.reshape(1, -1), kv_a_norm.reshape(1, -1), wq, wkv)


def _attn_kernel(q_ref, k_ref, v_ref, o_ref):
    s = lax.dot_general(q_ref[...], k_ref[...], (((1,), (1,)), ((), ())), preferred_element_type=F32)
    m = jnp.max(s, axis=-1, keepdims=True)
    p = jnp.exp2(s - m)
    l = jnp.sum(p, axis=-1, keepdims=True)
    o = jnp.dot(p.astype(BF16), v_ref[...], preferred_element_type=F32)
    o_ref[...] = (o / l).astype(o_ref.dtype)


def _attention(q, k, v, batch, seq, tq=512):
    t = q.shape[0]
    nq = seq // tq
    return pl.pallas_call(
        _attn_kernel,
        grid=(batch, MLA_HEADS, nq),
        in_specs=[pl.BlockSpec((tq, QK_PAD_DIM), lambda b, h, i: (b * nq + i, h)),
                  pl.BlockSpec((seq, QK_PAD_DIM), lambda b, h, i: (b, h)),
                  pl.BlockSpec((seq, V_HEAD_DIM), lambda b, h, i: (b, h))],
        out_specs=pl.BlockSpec((tq, V_HEAD_DIM), lambda b, h, i: (b * nq + i, h)),
        out_shape=jax.ShapeDtypeStruct((t, MLA_HEADS * V_HEAD_DIM), BF16),
        compiler_params=_params(("parallel", "parallel", "arbitrary")),
        name="mla_attention",
    )(q, k, v)


def _halo_specs(tm, halo, width, n_rows):
    per = tm // halo
    last = n_rows // halo - 1
    return [pl.BlockSpec((tm, width), lambda i: (i, 0)),
            pl.BlockSpec((halo, width), lambda i: (jnp.maximum(i * per - 1, 0), 0)),
            pl.BlockSpec((halo, width), lambda i: (jnp.minimum((i + 1) * per, last), 0))]


def _seq_edges(tiles_per_seq):
    i = pl.program_id(0) % tiles_per_seq
    return i == 0, i == tiles_per_seq - 1


CONF_HALO = 16
CONF_ROWS = 32


def _conformer_kernel(cur_ref, prev_ref, next_ref, w_ref, b_ref, g_ref, beta_ref, o_ref, ext_ref,
                      *, tiles_per_seq):
    tm = cur_ref.shape[0]
    c = CONV_CHANNELS
    first, last = _seq_edges(tiles_per_seq)

    def glu(u):
        return u[:, :c] * _sigmoid(u[:, c:])

    ext_ref[0:CONF_HALO, :] = jnp.where(first, 0.0, glu(prev_ref[...]))
    ext_ref[CONF_HALO:CONF_HALO + tm, :] = glu(cur_ref[...])
    ext_ref[CONF_HALO + tm:, :] = jnp.where(last, 0.0, glu(next_ref[...]))
    pad = CONV_WIDTH // 2
    bias = b_ref[...]
    for r in range(0, tm, CONF_ROWS):
        acc = jnp.broadcast_to(bias, (CONF_ROWS, c))
        for d in range(CONV_WIDTH):
            start = CONF_HALO + r - pad + d
            acc = acc + ext_ref[start:start + CONF_ROWS, :] * w_ref[d:d + 1, :]
        mu = jnp.mean(acc, axis=-1, keepdims=True)
        cen = acc - mu
        var = jnp.mean(cen * cen, axis=-1, keepdims=True)
        y = cen * lax.rsqrt(var + EPS) * g_ref[...] + beta_ref[...]
        o_ref[r:r + CONF_ROWS, :] = _silu(y).astype(o_ref.dtype)


def _conformer(conv_in, dw_w, dw_b, ln_g, ln_b, seq, tm=512):
    t = conv_in.shape[0]
    c = CONV_CHANNELS
    return pl.pallas_call(
        functools.partial(_conformer_kernel, tiles_per_seq=seq // tm),
        grid=(t // tm,),
        in_specs=_halo_specs(tm, CONF_HALO, 2 * c, t) + [
            _resident((CONV_WIDTH, c)), _resident((1, c)), _resident((1, c)), _resident((1, c))],
        out_specs=pl.BlockSpec((tm, c), lambda i: (i, 0)),
        out_shape=jax.ShapeDtypeStruct((t, c), BF16),
        scratch_shapes=[pltpu.VMEM((tm + 2 * CONF_HALO, c), F32)],
        compiler_params=_params(("parallel",)),
        name="conformer_conv",
    )(conv_in, conv_in, conv_in, dw_w.reshape(CONV_WIDTH, c), dw_b.reshape(1, c),
      ln_g.reshape(1, c), ln_b.reshape(1, c))


GDN_HALO = 8
GDN_ROWS = 64


def _gdn_prep_kernel(cur_ref, prev_ref, next_ref, misc_ref, w_ref, alog_ref, dtb_ref,
                     q_out, k_out, v_out, gate_out, ext_ref, *, tiles_per_seq):
    tm = cur_ref.shape[0]
    first, last = _seq_edges(tiles_per_seq)
    ext_ref[0:GDN_HALO, :] = jnp.where(first, 0.0, prev_ref[...])
    ext_ref[GDN_HALO:GDN_HALO + tm, :] = cur_ref[...]
    ext_ref[GDN_HALO + tm:, :] = jnp.where(last, 0.0, next_ref[...])
    pad = GDN_SHORT_CONV // 2
    for r in range(0, tm, GDN_ROWS):
        rows = slice(r, r + GDN_ROWS)
        for part, out in enumerate((q_out, k_out, v_out)):
            cols = slice(part * GDN_W, (part + 1) * GDN_W)
            acc = jnp.zeros((GDN_ROWS, GDN_W), F32)
            for d in range(GDN_SHORT_CONV):
                start = GDN_HALO + r - pad + d
                acc = acc + ext_ref[start:start + GDN_ROWS, cols] * w_ref[d:d + 1, cols]
            y = _silu(acc)
            if part == 2:
                out[rows, :] = y
                continue
            scale = GDN_K_DIM ** -0.5 if part == 0 else 1.0
            for h in range(GDN_HEADS):
                blk = slice(h * GDN_K_DIM, (h + 1) * GDN_K_DIM)
                yh = y[:, blk]
                inv = lax.rsqrt(jnp.sum(yh * yh, axis=-1, keepdims=True) + EPS)
                out[rows, blk] = yh * (inv * scale)
    m = misc_ref[...]
    lane = lax.broadcasted_iota(jnp.int32, m.shape, 1) - GATE_LANE0
    is_gate = (lane >= 0) & (lane < 4 * GDN_HEADS)
    is_a = is_gate & ((lane // GDN_HEADS) % 2 == 0)
    z = m + dtb_ref[...]
    softplus = jnp.maximum(z, 0.0) + jnp.log(1.0 + jnp.exp(-jnp.abs(z)))
    g = -jnp.exp(alog_ref[...]) * softplus
    gate_out[...] = jnp.where(is_a, g, jnp.where(is_gate, _sigmoid(m), 0.0))


def _gdn_prep(gqkv, misc, conv_w, a_log, dt_bias, seq, tm=512):
    t = gqkv.shape[0]
    w3 = 3 * GDN_W
    zeros = jnp.zeros((GDN_HEADS,), F32)
    lane_vals = lambda p: jnp.zeros((1, LANES), F32).at[0, GATE_LANE0:GATE_LANE0 + 4 * GDN_HEADS].set(
        jnp.concatenate([p[0], zeros, p[1], zeros]))
    row = lambda n: pl.BlockSpec((tm, n), lambda i: (i, 0))
    return pl.pallas_call(
        functools.partial(_gdn_prep_kernel, tiles_per_seq=seq // tm),
        grid=(t // tm,),
        in_specs=_halo_specs(tm, GDN_HALO, w3, t) + [
            row(LANES), _resident((GDN_SHORT_CONV, w3)), _resident((1, LANES)), _resident((1, LANES))],
        out_specs=[row(GDN_W), row(GDN_W), row(GDN_W), row(LANES)],
        out_shape=[jax.ShapeDtypeStruct((t, GDN_W), F32)] * 3 + [jax.ShapeDtypeStruct((t, LANES), F32)],
        scratch_shapes=[pltpu.VMEM((tm + 2 * GDN_HALO, w3), F32)],
        compiler_params=_params(("parallel",)),
        name="gdn_prep",
    )(gqkv, gqkv, gqkv, misc, conv_w.reshape(GDN_SHORT_CONV, w3), lane_vals(a_log), lane_vals(dt_bias))


def _bdot(a, b):
    return jnp.dot(a.astype(BF16), b.astype(BF16), preferred_element_type=F32)


def _bdot_nt(a, b):
    return lax.dot_general(a.astype(BF16), b.astype(BF16), (((1,), (1,)), ((), ())),
                           preferred_element_type=F32)


def _delta_chunk(q, k, v, kk, qk, beta, gc, gc_row, g_tot, lower, state):
    c = q.shape[0]
    ri = lax.broadcasted_iota(jnp.int32, (c, c), 0)
    ci = lax.broadcasted_iota(jnp.int32, (c, c), 1)
    incl = (ri >= ci) if lower else (ri <= ci)
    strict = (ri > ci) if lower else (ri < ci)
    decay = jnp.where(incl, jnp.exp(jnp.where(incl, gc - gc_row, 0.0)), 0.0)
    a = jnp.where(strict, kk * beta * decay, 0.0)
    xor = ri ^ ci
    eye = jnp.where(ri == ci, 1.0, 0.0)
    t_inv = eye - jnp.where(xor < 2, a, 0.0)
    b = 2
    while b < c:
        e = jnp.where((xor >= b) & (xor < 2 * b), a, 0.0)
        t_inv = t_inv - _bdot(t_inv, _bdot(e, t_inv))
        b *= 2
    rhs = jnp.concatenate([v * beta, k * (beta * jnp.exp(gc))], axis=1)
    uw = _bdot(t_inv, rhs)
    d = q.shape[1]
    z = _bdot(qk * decay, uw)
    k_til = k * jnp.exp(g_tot - gc)
    y = _bdot(k_til.T, uw)
    q_eff = q * jnp.exp(gc) - z[:, d:]
    o = _bdot(q_eff, state) + z[:, :d]
    new_state = state * jnp.exp(g_tot) - _bdot(y[:, d:], state) + y[:, :d]
    return o, new_state


def _gdn_kernel(qf_ref, kf_ref, vf_ref, gf_ref, qb_ref, kb_ref, vb_ref, gb_ref, of_ref, ob_ref,
                state_ref, *, heads):
    c = qf_ref.shape[0]

    @pl.when(pl.program_id(2) == 0)
    def _():
        state_ref[...] = jnp.zeros_like(state_ref)

    ri = lax.broadcasted_iota(jnp.int32, (c, c), 0)
    ci = lax.broadcasted_iota(jnp.int32, (c, c), 1)
    for direction, (q_ref, k_ref, v_ref, g_ref, o_ref) in enumerate(
            ((qf_ref, kf_ref, vf_ref, gf_ref, of_ref), (qb_ref, kb_ref, vb_ref, gb_ref, ob_ref))):
        lower = direction == 0
        gates = g_ref[...]
        tri = jnp.where((ri >= ci) if lower else (ri <= ci), 1.0, 0.0)
        gc_all = jnp.dot(tri, gates, preferred_element_type=F32, precision=lax.Precision.HIGHEST)
        gc_all_t = gc_all.T
        last = c - 1 if lower else 0
        for h in range(heads):
            blk = slice(h * GDN_K_DIM, (h + 1) * GDN_K_DIM)
            g_lane = GATE_LANE0 + 2 * GDN_HEADS * direction + pl.program_id(1) * heads + h
            q, k, v = q_ref[:, blk], k_ref[:, blk], v_ref[:, blk]
            kk = _bdot_nt(k, k)
            qk = _bdot_nt(q, k)
            onehot_g = lax.broadcasted_iota(jnp.int32, (1, LANES), 1) == g_lane
            onehot_b = lax.broadcasted_iota(jnp.int32, (1, LANES), 1) == g_lane + GDN_HEADS
            gc = jnp.sum(jnp.where(onehot_g, gc_all, 0.0), axis=1, keepdims=True)
            beta = jnp.sum(jnp.where(onehot_b, gates, 0.0), axis=1, keepdims=True)
            onehot_col = lax.broadcasted_iota(jnp.int32, (LANES, 1), 0) == g_lane
            gc_row = jnp.sum(jnp.where(onehot_col, gc_all_t, 0.0), axis=0, keepdims=True)
            g_tot = gc[last:last + 1, :]
            o, new_state = _delta_chunk(q, k, v, kk, qk, beta, gc, gc_row, g_tot, lower,
                                        state_ref[direction, h])
            o_ref[:, blk] = o
            state_ref[direction, h] = new_state


def _gdn(q, k, v, gates, batch, seq, heads=2):
    t = q.shape[0]
    c = GDN_CHUNK
    nc = seq // c
    hw = heads * GDN_K_DIM
    gate_f = pl.BlockSpec((c, LANES), lambda b, hp, i: (b * nc + i, 0))
    gate_b = pl.BlockSpec((c, LANES), lambda b, hp, i: (b * nc + nc - 1 - i, 0))
    head_f = pl.BlockSpec((c, hw), lambda b, hp, i: (b * nc + i, hp))
    head_b = pl.BlockSpec((c, hw), lambda b, hp, i: (b * nc + nc - 1 - i, hp))
    return pl.pallas_call(
        functools.partial(_gdn_kernel, heads=heads),
        grid=(batch, GDN_HEADS // heads, nc),
        in_specs=[head_f, head_f, head_f, gate_f, head_b, head_b, head_b, gate_b],
        out_specs=[head_f, head_b],
        out_shape=[jax.ShapeDtypeStruct((t, GDN_W), F32)] * 2,
        scratch_shapes=[pltpu.VMEM((2, heads, GDN_K_DIM, GDN_V_DIM), F32)],
        compiler_params=_params(("parallel", "parallel", "arbitrary")),
        name="gdn_delta_rule",
    )(q, k, v, gates, q, k, v, gates)


def _out_proj_kernel(x_ref, oa_ref, ob_ref, of_ref, obw_ref, z_ref, onorm_ref, w_ref, g_ref, o_ref):
    oc = of_ref[...] + obw_ref[...]
    z = z_ref[...]
    parts = []
    for h in range(GDN_HEADS):
        blk = slice(h * GDN_V_DIM, (h + 1) * GDN_V_DIM)
        parts.append(_rms(oc[:, blk], onorm_ref[...]) * _silu(z[:, blk]))
    og = jnp.concatenate(parts, axis=1).astype(BF16)
    na = oa_ref.shape[1]
    nb = ob_ref.shape[1]
    mix = (jnp.dot(oa_ref[...], w_ref[:na, :], preferred_element_type=F32)
           + jnp.dot(ob_ref[...], w_ref[na:na + nb, :], preferred_element_type=F32)
           + jnp.dot(og, w_ref[na + nb:, :], preferred_element_type=F32))
    o_ref[...] = x_ref[...] + _rms(mix, g_ref[...])


def _out_proj(x, o_a, o_b, o_f, o_bw, z, out_norm, w_out, gain, tm=512):
    t, d = x.shape
    row = lambda n: pl.BlockSpec((tm, n), lambda i: (i, 0))
    return pl.pallas_call(
        _out_proj_kernel,
        grid=(t // tm,),
        in_specs=[row(d), row(o_a.shape[1]), row(o_b.shape[1]), row(GDN_W), row(GDN_W), row(GDN_W),
                  _resident((1, GDN_V_DIM)), _resident(w_out.shape), _resident((1, d))],
        out_specs=row(d),
        out_shape=jax.ShapeDtypeStruct((t, d), F32),
        compiler_params=_params(("parallel",)),
        name="out_proj",
    )(x, o_a, o_b, o_f, o_bw, z, out_norm.reshape(1, -1), w_out, gain.reshape(1, d))


def _ffn_kernel(x_ref, gin_ref, wg_ref, wu_ref, wd_ref, gout_ref, o_ref, hn_ref):
    j = pl.program_id(1)

    @pl.when(j == 0)
    def _():
        hn_ref[...] = _rms(x_ref[...], gin_ref[...]).astype(BF16)

    hn = hn_ref[...]
    gate = jnp.dot(hn, wg_ref[...], preferred_element_type=F32)
    up = jnp.dot(hn, wu_ref[...], preferred_element_type=F32)
    part = jnp.dot((_silu(gate) * up).astype(BF16), wd_ref[...], preferred_element_type=F32)

    @pl.when(j == 0)
    def _():
        o_ref[...] = part

    @pl.when(j > 0)
    def _():
        o_ref[...] += part

    @pl.when(j == pl.num_programs(1) - 1)
    def _():
        o_ref[...] = x_ref[...] + _rms(o_ref[...], gout_ref[...])


def _ffn(x, gain_in, w_gate, w_up, w_down, gain_out, tm=1024, tf=512):
    t, d = x.shape
    f = w_gate.shape[1]
    tm = min(tm, t)
    return pl.pallas_call(
        _ffn_kernel,
        grid=(t // tm, f // tf),
        in_specs=[pl.BlockSpec((tm, d), lambda i, j: (i, 0), pipeline_mode=pl.Buffered(1)),
                  _resident((1, d)),
                  pl.BlockSpec((d, tf), lambda i, j: (0, j)),
                  pl.BlockSpec((d, tf), lambda i, j: (0, j)),
                  pl.BlockSpec((tf, d), lambda i, j: (j, 0)),
                  _resident((1, d))],
        out_specs=pl.BlockSpec((tm, d), lambda i, j: (i, 0)),
        out_shape=jax.ShapeDtypeStruct((t, d), F32),
        scratch_shapes=[pltpu.VMEM((tm, d), BF16)],
        compiler_params=_params(("parallel", "arbitrary")),
        name="swiglu_ffn",
    )(x, gain_in.reshape(1, d), w_gate, w_up, w_down, gain_out.reshape(1, d))


def _rot_half_cols(w):
    half = w.shape[-1] // 2
    return jnp.concatenate([-w[..., half:], w[..., :half]], axis=-1)


def _pad_cols(w, n):
    return jnp.pad(w, [(0, 0)] * (w.ndim - 1) + [(0, n - w.shape[-1])])


IN_WIDTHS = (Q_LORA_RANK, KV_LORA_RANK, 2 * CONV_CHANNELS, 3 * GDN_W, GDN_W, LANES, LANES)


def _layout_w_in(w_in):
    sizes = (Q_LORA_RANK, KV_LORA_RANK, QK_ROPE_DIM, 2 * CONV_CHANNELS, 3 * GDN_W, GDN_W, 4 * GDN_HEADS)
    c_q, c_kv, k_rope, conv, gqkv, gz, gates = jnp.split(w_in, np.cumsum(sizes)[:-1].tolist(), axis=1)
    misc = _pad_cols(jnp.concatenate([k_rope, gates], axis=1), LANES)
    krot = _pad_cols(_rot_half_cols(k_rope), LANES)
    return jnp.concatenate([c_q, c_kv, conv, gqkv, gz, misc, krot], axis=1).astype(BF16)


def _layout_w_uq(w_uq):
    r = w_uq.shape[0]
    w = w_uq.reshape(r, MLA_HEADS, QK_HEAD_DIM)
    nope = w[..., :QK_NOPE_DIM]
    rope = w[..., QK_NOPE_DIM:]
    parts = [nope, _pad_cols(rope, LANES), _pad_cols(_rot_half_cols(rope), LANES)]
    return jnp.concatenate([p.reshape(r, MLA_HEADS * LANES) for p in parts], axis=1).astype(BF16)


def _layout_w_ukv(w_ukv):
    r = w_ukv.shape[0]
    w = w_ukv.reshape(r, MLA_HEADS, QK_NOPE_DIM + V_HEAD_DIM)
    return jnp.concatenate([w[..., :QK_NOPE_DIM].reshape(r, -1), w[..., QK_NOPE_DIM:].reshape(r, -1)],
                           axis=1).astype(BF16)


def _layer(x, cos, sin, batch, seq, pre_mix_norm, w_in, q_a_norm, w_uq, kv_a_norm, w_ukv, conv_dw_w,
           conv_dw_b, conv_ln_g, conv_ln_b, gdn_conv_w, gdn_a_log, gdn_dt_bias, gdn_out_norm, w_out,
           post_mix_norm, pre_ffn_norm, w_gate, w_up, w_down, post_ffn_norm):
    cq, ckv, conv_in, gqkv, gz, misc, krot = _in_proj(x, pre_mix_norm, _layout_w_in(w_in), IN_WIDTHS)
    q_scale = QK_HEAD_DIM ** -0.5 * float(np.log2(np.e))
    q, k, v = _mla_prep(cq, ckv, misc, krot, cos, sin, q_a_norm, kv_a_norm,
                        _layout_w_uq(w_uq), _layout_w_ukv(w_ukv), q_scale)
    o_a = _attention(q, k, v, batch, seq)
    o_b = _conformer(conv_in, conv_dw_w, conv_dw_b, conv_ln_g, conv_ln_b, seq)
    gq, gk, gv, gates = _gdn_prep(gqkv, misc, gdn_conv_w, gdn_a_log, gdn_dt_bias, seq)
    o_f, o_bw = _gdn(gq, gk, gv, gates, batch, seq)
    x = _out_proj(x, o_a, o_b, o_f, o_bw, gz, gdn_out_norm, w_out.astype(BF16), post_mix_norm)
    return _ffn(x, pre_ffn_norm, w_gate.astype(BF16), w_up.astype(BF16), w_down.astype(BF16),
                post_ffn_norm)


def kernel(x, positions, pre_mix_norm, w_in, q_a_norm, w_uq, kv_a_norm, w_ukv, conv_dw_w, conv_dw_b,
           conv_ln_g, conv_ln_b, gdn_conv_w, gdn_a_log, gdn_dt_bias, gdn_out_norm, w_out, post_mix_norm,
           pre_ffn_norm, w_gate, w_up, w_down, post_ffn_norm):
    batch, seq, d = x.shape
    cos, sin = _rope_tables(positions)
    h = x.reshape(batch * seq, d)
    per_layer = (pre_mix_norm, w_in, q_a_norm, w_uq, kv_a_norm, w_ukv, conv_dw_w, conv_dw_b, conv_ln_g,
                 conv_ln_b, gdn_conv_w, gdn_a_log, gdn_dt_bias, gdn_out_norm, w_out, post_mix_norm,
                 pre_ffn_norm, w_gate, w_up, w_down, post_ffn_norm)
    for l in range(pre_mix_norm.shape[0]):
        h = _layer(h, cos, sin, batch, seq, *(p[l] for p in per_layer))
    return h.reshape(batch, seq, d)
```

```python
import functools

import numpy as np
import jax
import jax.numpy as jnp
from jax import lax
from jax.experimental import pallas as pl
from jax.experimental.pallas import tpu as pltpu

F32 = jnp.float32
BF16 = jnp.bfloat16

EPS = 1e-6
LANES = 128
SUBLANES = 8
VMEM_LIMIT = 56 * 1024 * 1024

MLA_HEADS = 8
Q_LORA_RANK = 512
KV_LORA_RANK = 512
QK_NOPE_DIM = 128
QK_ROPE_DIM = 64
QK_HEAD_DIM = QK_NOPE_DIM + QK_ROPE_DIM
V_HEAD_DIM = 128
ROPE_THETA = 10000.0
QK_PAD_DIM = 2 * LANES
V_PAD_DIM = 2 * LANES
CONV_CHANNELS = 512
CONV_WIDTH = 31
GDN_HEADS = 4
GDN_K_DIM = 128
GDN_V_DIM = 128
GDN_SHORT_CONV = 5
GDN_W = GDN_HEADS * GDN_K_DIM
GDN_CHUNK = 256
GATE_LANE0 = QK_ROPE_DIM


def _params(semantics):
    return pltpu.CompilerParams(dimension_semantics=semantics, vmem_limit_bytes=VMEM_LIMIT)


def _resident(shape):
    return pl.BlockSpec(shape, lambda *_: (0,) * len(shape), pipeline_mode=pl.Buffered(1))


def _rms(x, w):
    return x * lax.rsqrt(jnp.mean(x * x, axis=-1, keepdims=True) + EPS) * w


def _sigmoid(x):
    return 1.0 / (1.0 + jnp.exp(-x))


def _silu(x):
    return x * _sigmoid(x)


def _rope_table_kernel(pos_ref, inv_ref, cos_ref, sin_ref):
    ang = pos_ref[...].astype(F32) * inv_ref[...]
    lane = lax.broadcasted_iota(jnp.int32, ang.shape, 1)
    valid = lane < QK_ROPE_DIM
    cos_ref[...] = jnp.where(valid, jnp.cos(ang), 0.0)
    sin_ref[...] = jnp.where(valid, jnp.sin(ang), 0.0)


def _rope_tables(positions):
    t = positions.size
    tm = min(t, 2048)
    half = QK_ROPE_DIM // 2
    inv = 1.0 / (ROPE_THETA ** (jnp.arange(0, QK_ROPE_DIM, 2, dtype=F32) / QK_ROPE_DIM))
    inv_row = jnp.zeros((1, LANES), F32).at[0, :half].set(inv).at[0, half:2 * half].set(inv)
    spec = pl.BlockSpec((tm, LANES), lambda i: (i, 0))
    return pl.pallas_call(
        _rope_table_kernel,
        grid=(t // tm,),
        in_specs=[pl.BlockSpec((tm, 1), lambda i: (i, 0)), _resident((1, LANES))],
        out_specs=[spec, spec],
        out_shape=[jax.ShapeDtypeStruct((t, LANES), F32)] * 2,
        compiler_params=_params(("parallel",)),
        name="rope_tables",
    )(positions.reshape(t, 1), inv_row)


def _in_proj_kernel(x_ref, g_ref, w_ref, *out_refs):
    xn = _rms(x_ref[...], g_ref[...]).astype(BF16)
    off = 0
    for o_ref in out_refs:
        n = o_ref.shape[1]
        o_ref[...] = jnp.dot(xn, w_ref[:, off:off + n], preferred_element_type=F32).astype(o_ref.dtype)
        off += n


def _in_proj(x, gain, w, widths, tm=512):
    t, d = x.shape
    n = w.shape[1]
    assert sum(widths) == n
    return pl.pallas_call(
        _in_proj_kernel,
        grid=(t // tm,),
        in_specs=[pl.BlockSpec((tm, d), lambda i: (i, 0)), _resident((1, d)), _resident((d, n))],
        out_specs=[pl.BlockSpec((tm, wd), lambda i: (i, 0)) for wd in widths],
        out_shape=[jax.ShapeDtypeStruct((t, wd), F32) for wd in widths],
        compiler_params=_params(("parallel",)),
        name="in_proj",
    )(x, gain.reshape(1, d), w)


def _mla_prep_kernel(cq_ref, ckv_ref, misc_ref, krot_ref, cos_ref, sin_ref, qn_ref, kvn_ref,
                     wq_ref, wkv_ref, q_out, k_out, v_out, *, q_scale):
    cos = cos_ref[...]
    sin = sin_ref[...]
    hw = MLA_HEADS * LANES
    cqn = _rms(cq_ref[...], qn_ref[...]).astype(BF16)
    q_nope = jnp.dot(cqn, wq_ref[:, :hw], preferred_element_type=F32)
    q_rope = jnp.dot(cqn, wq_ref[:, hw:2 * hw], preferred_element_type=F32)
    q_rot = jnp.dot(cqn, wq_ref[:, 2 * hw:], preferred_element_type=F32)
    ckvn = _rms(ckv_ref[...], kvn_ref[...]).astype(BF16)
    k_nope = jnp.dot(ckvn, wkv_ref[:, :hw], preferred_element_type=F32)
    v = jnp.dot(ckvn, wkv_ref[:, hw:], preferred_element_type=F32)
    k_rope = (misc_ref[...] * cos + krot_ref[...] * sin).astype(k_out.dtype)
    ones_col = jnp.where(lax.broadcasted_iota(jnp.int32, cos.shape, 1) == 0, 1.0, 0.0).astype(v_out.dtype)
    for h in range(MLA_HEADS):
        v_out[:, h * V_PAD_DIM:h * V_PAD_DIM + V_HEAD_DIM] = (
            v[:, h * V_HEAD_DIM:(h + 1) * V_HEAD_DIM].astype(v_out.dtype))
        v_out[:, h * V_PAD_DIM + V_HEAD_DIM:(h + 1) * V_PAD_DIM] = ones_col
        blk = slice(h * LANES, (h + 1) * LANES)
        lo = slice(h * QK_PAD_DIM, h * QK_PAD_DIM + LANES)
        hi = slice(h * QK_PAD_DIM + LANES, (h + 1) * QK_PAD_DIM)
        q_out[:, lo] = (q_nope[:, blk] * q_scale).astype(q_out.dtype)
        q_out[:, hi] = ((q_rope[:, blk] * cos + q_rot[:, blk] * sin) * q_scale).astype(q_out.dtype)
        k_out[:, lo] = k_nope[:, blk].astype(k_out.dtype)
        k_out[:, hi] = k_rope


def _mla_prep(cq, ckv, misc, krot, cos, sin, q_a_norm, kv_a_norm, wq, wkv, q_scale, tm=512):
    t = cq.shape[0]
    row = lambda n: pl.BlockSpec((tm, n), lambda i: (i, 0))
    qk_w = MLA_HEADS * QK_PAD_DIM
    v_w = MLA_HEADS * V_PAD_DIM
    return pl.pallas_call(
        functools.partial(_mla_prep_kernel, q_scale=q_scale),
        grid=(t // tm,),
        in_specs=[row(Q_LORA_RANK), row(KV_LORA_RANK), row(LANES), row(LANES), row(LANES), row(LANES),
                  _resident((1, Q_LORA_RANK)), _resident((1, KV_LORA_RANK)),
                  _resident(wq.shape), _resident(wkv.shape)],
        out_specs=[row(qk_w), row(qk_w), row(v_w)],
        out_shape=[jax.ShapeDtypeStruct((t, qk_w), BF16), jax.ShapeDtypeStruct((t, qk_w), BF16),
                   jax.ShapeDtypeStruct((t, v_w), BF16)],
        compiler_params=_params(("parallel",)),
        name="mla_prep",
    )(cq, ckv, misc, krot, cos, sin, q_a_norm.reshape(1, -1), kv_a_norm.reshape(1, -1), wq, wkv)


ATTN_TK = 512


def _attn_kernel(q_ref, k_ref, v_ref, o_ref):
    q = q_ref[...]
    tq = q.shape[0]
    m = jnp.full((tq, 1), -jnp.inf, F32)
    acc = jnp.zeros((tq, v_ref.shape[1]), F32)
    for j in range(k_ref.shape[0] // ATTN_TK):
        rows = slice(j * ATTN_TK, (j + 1) * ATTN_TK)
        s = lax.dot_general(q, k_ref[rows, :], (((1,), (1,)), ((), ())), preferred_element_type=F32)
        m_new = jnp.maximum(m, jnp.max(s, axis=-1, keepdims=True))
        p = jnp.exp2(s - m_new).astype(BF16)
        acc = acc * jnp.exp2(m - m_new) + jnp.dot(p, v_ref[rows, :], preferred_element_type=F32)
        m = m_new
    o_ref[...] = (acc[:, :V_HEAD_DIM] / acc[:, V_HEAD_DIM:V_HEAD_DIM + 1]).astype(o_ref.dtype)


def _attention(q, k, v, batch, seq, tq=512):
    t = q.shape[0]
    nq = seq // tq
    return pl.pallas_call(
        _attn_kernel,
        grid=(batch, MLA_HEADS, nq),
        in_specs=[pl.BlockSpec((tq, QK_PAD_DIM), lambda b, h, i: (b * nq + i, h)),
                  pl.BlockSpec((seq, QK_PAD_DIM), lambda b, h, i: (b, h)),
                  pl.BlockSpec((seq, V_PAD_DIM), lambda b, h, i: (b, h))],
        out_specs=pl.BlockSpec((tq, V_HEAD_DIM), lambda b, h, i: (b * nq + i, h)),
        out_shape=jax.ShapeDtypeStruct((t, MLA_HEADS * V_HEAD_DIM), BF16),
        compiler_params=_params(("parallel", "parallel", "arbitrary")),
        name="mla_attention",
    )(q, k, v)


def _halo_specs(tm, halo, width, n_rows):
    per = tm // halo
    last = n_rows // halo - 1
    return [pl.BlockSpec((tm, width), lambda i: (i, 0)),
            pl.BlockSpec((halo, width), lambda i: (jnp.maximum(i * per - 1, 0), 0)),
            pl.BlockSpec((halo, width), lambda i: (jnp.minimum((i + 1) * per, last), 0))]


def _seq_edges(tiles_per_seq):
    i = pl.program_id(0) % tiles_per_seq
    return i == 0, i == tiles_per_seq - 1


CONF_HALO = 16
CONF_ROWS = 32


def _conformer_kernel(cur_ref, prev_ref, next_ref, w_ref, b_ref, g_ref, beta_ref, o_ref, ext_ref,
                      shift_ref, *, tiles_per_seq):
    tm = cur_ref.shape[0]
    c = CONV_CHANNELS
    first, last = _seq_edges(tiles_per_seq)

    def glu(u):
        return u[:, :c] * _sigmoid(u[:, c:])

    ext_ref[0:CONF_HALO, :] = jnp.where(first, 0.0, glu(prev_ref[...]))
    ext_ref[CONF_HALO:CONF_HALO + tm, :] = glu(cur_ref[...])
    ext_ref[CONF_HALO + tm:, :] = jnp.where(last, 0.0, glu(next_ref[...]))
    n_shift = shift_ref.shape[1]
    for s in range(1, SUBLANES):
        shift_ref[s, :, :] = ext_ref[s:s + n_shift, :]
    pad = CONV_WIDTH // 2
    bias = b_ref[...]
    for r in range(0, tm, CONF_ROWS):
        acc = jnp.broadcast_to(bias, (CONF_ROWS, c))
        for d in range(CONV_WIDTH):
            start = CONF_HALO + r - pad + d
            s = start % SUBLANES
            base = start - s
            win = ext_ref[base:base + CONF_ROWS, :] if s == 0 else shift_ref[s, base:base + CONF_ROWS, :]
            acc = acc + win * w_ref[d:d + 1, :]
        mu = jnp.mean(acc, axis=-1, keepdims=True)
        cen = acc - mu
        var = jnp.mean(cen * cen, axis=-1, keepdims=True)
        y = cen * lax.rsqrt(var + EPS) * g_ref[...] + beta_ref[...]
        o_ref[r:r + CONF_ROWS, :] = _silu(y).astype(o_ref.dtype)


def _conformer(conv_in, dw_w, dw_b, ln_g, ln_b, seq, tm=512):
    t = conv_in.shape[0]
    c = CONV_CHANNELS
    return pl.pallas_call(
        functools.partial(_conformer_kernel, tiles_per_seq=seq // tm),
        grid=(t // tm,),
        in_specs=_halo_specs(tm, CONF_HALO, 2 * c, t) + [
            _resident((CONV_WIDTH, c)), _resident((1, c)), _resident((1, c)), _resident((1, c))],
        out_specs=pl.BlockSpec((tm, c), lambda i: (i, 0)),
        out_shape=jax.ShapeDtypeStruct((t, c), BF16),
        scratch_shapes=[pltpu.VMEM((tm + 2 * CONF_HALO, c), F32),
                        pltpu.VMEM((SUBLANES, tm + 2 * CONF_HALO - SUBLANES, c), F32)],
        compiler_params=_params(("parallel",)),
        name="conformer_conv",
    )(conv_in, conv_in, conv_in, dw_w.reshape(CONV_WIDTH, c), dw_b.reshape(1, c),
      ln_g.reshape(1, c), ln_b.reshape(1, c))


GDN_HALO = 8
GDN_ROWS = 64


def _gdn_prep_kernel(cur_ref, prev_ref, next_ref, misc_ref, w_ref, alog_ref, dtb_ref,
                     q_out, k_out, v_out, gate_out, ext_ref, *, tiles_per_seq):
    tm = cur_ref.shape[0]
    first, last = _seq_edges(tiles_per_seq)
    ext_ref[0:GDN_HALO, :] = jnp.where(first, 0.0, prev_ref[...])
    ext_ref[GDN_HALO:GDN_HALO + tm, :] = cur_ref[...]
    ext_ref[GDN_HALO + tm:, :] = jnp.where(last, 0.0, next_ref[...])
    pad = GDN_SHORT_CONV // 2
    for r in range(0, tm, GDN_ROWS):
        rows = slice(r, r + GDN_ROWS)
        for part, out in enumerate((q_out, k_out, v_out)):
            cols = slice(part * GDN_W, (part + 1) * GDN_W)
            acc = jnp.zeros((GDN_ROWS, GDN_W), F32)
            for d in range(GDN_SHORT_CONV):
                start = GDN_HALO + r - pad + d
                acc = acc + ext_ref[start:start + GDN_ROWS, cols] * w_ref[d:d + 1, cols]
            y = _silu(acc)
            if part == 2:
                out[rows, :] = y
                continue
            scale = GDN_K_DIM ** -0.5 if part == 0 else 1.0
            for h in range(GDN_HEADS):
                blk = slice(h * GDN_K_DIM, (h + 1) * GDN_K_DIM)
                yh = y[:, blk]
                inv = lax.rsqrt(jnp.sum(yh * yh, axis=-1, keepdims=True) + EPS)
                out[rows, blk] = yh * (inv * scale)
    m = misc_ref[...]
    lane = lax.broadcasted_iota(jnp.int32, m.shape, 1) - GATE_LANE0
    is_gate = (lane >= 0) & (lane < 4 * GDN_HEADS)
    is_a = is_gate & ((lane // GDN_HEADS) % 2 == 0)
    z = m + dtb_ref[...]
    softplus = jnp.maximum(z, 0.0) + jnp.log(1.0 + jnp.exp(-jnp.abs(z)))
    g = -jnp.exp(alog_ref[...]) * softplus
    gate_out[...] = jnp.where(is_a, g, jnp.where(is_gate, _sigmoid(m), 0.0))


def _gdn_prep(gqkv, misc, conv_w, a_log, dt_bias, seq, tm=512):
    t = gqkv.shape[0]
    w3 = 3 * GDN_W
    zeros = jnp.zeros((GDN_HEADS,), F32)
    lane_vals = lambda p: jnp.zeros((1, LANES), F32).at[0, GATE_LANE0:GATE_LANE0 + 4 * GDN_HEADS].set(
        jnp.concatenate([p[0], zeros, p[1], zeros]))
    row = lambda n: pl.BlockSpec((tm, n), lambda i: (i, 0))
    return pl.pallas_call(
        functools.partial(_gdn_prep_kernel, tiles_per_seq=seq // tm),
        grid=(t // tm,),
        in_specs=_halo_specs(tm, GDN_HALO, w3, t) + [
            row(LANES), _resident((GDN_SHORT_CONV, w3)), _resident((1, LANES)), _resident((1, LANES))],
        out_specs=[row(GDN_W), row(GDN_W), row(GDN_W), row(LANES)],
        out_shape=[jax.ShapeDtypeStruct((t, GDN_W), F32)] * 3 + [jax.ShapeDtypeStruct((t, LANES), F32)],
        scratch_shapes=[pltpu.VMEM((tm + 2 * GDN_HALO, w3), F32)],
        compiler_params=_params(("parallel",)),
        name="gdn_prep",
    )(gqkv, gqkv, gqkv, misc, conv_w.reshape(GDN_SHORT_CONV, w3), lane_vals(a_log), lane_vals(dt_bias))


def _bdot(a, b):
    return jnp.dot(a.astype(BF16), b.astype(BF16), preferred_element_type=F32)


def _bdot_nt(a, b):
    return lax.dot_general(a.astype(BF16), b.astype(BF16), (((1,), (1,)), ((), ())),
                           preferred_element_type=F32)


def _gdn_kernel(qf_ref, kf_ref, vf_ref, gf_ref, qb_ref, kb_ref, vb_ref, gb_ref, of_ref, ob_ref,
                state_ref, *, heads):
    c = qf_ref.shape[0]
    d = GDN_K_DIM

    @pl.when(pl.program_id(2) == 0)
    def _():
        state_ref[...] = jnp.zeros_like(state_ref)

    ri = lax.broadcasted_iota(jnp.int32, (c, c), 0)
    ci = lax.broadcasted_iota(jnp.int32, (c, c), 1)
    xor = ri ^ ci
    lane_row = lax.broadcasted_iota(jnp.int32, (1, LANES), 1)
    lane_col = lax.broadcasted_iota(jnp.int32, (LANES, 1), 0)

    chains = []
    for direction, (q_ref, k_ref, v_ref, g_ref, o_ref) in enumerate(
            ((qf_ref, kf_ref, vf_ref, gf_ref, of_ref), (qb_ref, kb_ref, vb_ref, gb_ref, ob_ref))):
        lower = direction == 0
        incl = (ri >= ci) if lower else (ri <= ci)
        strict = (ri > ci) if lower else (ri < ci)
        gates = g_ref[...]
        gc_all = jnp.dot(jnp.where(incl, 1.0, 0.0), gates, preferred_element_type=F32,
                         precision=lax.Precision.HIGHEST)
        gc_all_t = gc_all.T
        last = c - 1 if lower else 0
        for h in range(heads):
            blk = slice(h * d, (h + 1) * d)
            g_lane = GATE_LANE0 + 2 * GDN_HEADS * direction + pl.program_id(1) * heads + h
            gc = jnp.sum(jnp.where(lane_row == g_lane, gc_all, 0.0), axis=1, keepdims=True)
            beta = jnp.sum(jnp.where(lane_row == g_lane + GDN_HEADS, gates, 0.0), axis=1, keepdims=True)
            gc_row = jnp.sum(jnp.where(lane_col == g_lane, gc_all_t, 0.0), axis=0, keepdims=True)
            chains.append(dict(q=q_ref[:, blk], k=k_ref[:, blk], v=v_ref[:, blk], gc=gc, beta=beta,
                               gc_row=gc_row, g_tot=gc[last:last + 1, :], incl=incl, strict=strict,
                               o_ref=o_ref, blk=blk, idx=(direction, h)))

    for ch in chains:
        ch["kk"] = _bdot_nt(ch["k"], ch["k"])
    for ch in chains:
        ch["qk"] = _bdot_nt(ch["q"], ch["k"])
    for ch in chains:
        incl = ch["incl"]
        decay = jnp.where(incl, jnp.exp(jnp.where(incl, ch["gc"] - ch["gc_row"], 0.0)), 0.0)
        ch["a"] = jnp.where(ch["strict"], ch["kk"] * ch["beta"] * decay, 0.0)
        ch["qkd"] = (ch["qk"] * decay).astype(BF16)
        ch["t"] = jnp.where(ri == ci, 1.0, 0.0) - jnp.where(xor < 2, ch["a"], 0.0)
    b = 2
    while b < c:
        level = (xor >= b) & (xor < 2 * b)
        for ch in chains:
            ch["x"] = _bdot(jnp.where(level, ch["a"], 0.0), ch["t"])
        for ch in chains:
            ch["t"] = ch["t"] - _bdot(ch["t"], ch["x"])
        b *= 2
    for ch in chains:
        rhs = jnp.concatenate([ch["v"] * ch["beta"], ch["k"] * (ch["beta"] * jnp.exp(ch["gc"]))], axis=1)
        ch["uw"] = _bdot(ch["t"], rhs).astype(BF16)
    for ch in chains:
        ch["z"] = _bdot(ch["qkd"], ch["uw"])
    for ch in chains:
        k_til = ch["k"] * jnp.exp(ch["g_tot"] - ch["gc"])
        ch["y"] = _bdot(k_til.T, ch["uw"])
    for ch in chains:
        state = state_ref[ch["idx"]]
        q_eff = ch["q"] * jnp.exp(ch["gc"]) - ch["z"][:, d:]
        ch["o_ref"][:, ch["blk"]] = _bdot(q_eff, state) + ch["z"][:, :d]
        state_ref[ch["idx"]] = (state * jnp.exp(ch["g_tot"]) - _bdot(ch["y"][:, d:], state)
                                + ch["y"][:, :d])


def _gdn(q, k, v, gates, batch, seq, heads=GDN_HEADS):
    t = q.shape[0]
    c = GDN_CHUNK
    nc = seq // c
    hw = heads * GDN_K_DIM
    gate_f = pl.BlockSpec((c, LANES), lambda b, hp, i: (b * nc + i, 0))
    gate_b = pl.BlockSpec((c, LANES), lambda b, hp, i: (b * nc + nc - 1 - i, 0))
    head_f = pl.BlockSpec((c, hw), lambda b, hp, i: (b * nc + i, hp))
    head_b = pl.BlockSpec((c, hw), lambda b, hp, i: (b * nc + nc - 1 - i, hp))
    return pl.pallas_call(
        functools.partial(_gdn_kernel, heads=heads),
        grid=(batch, GDN_HEADS // heads, nc),
        in_specs=[head_f, head_f, head_f, gate_f, head_b, head_b, head_b, gate_b],
        out_specs=[head_f, head_b],
        out_shape=[jax.ShapeDtypeStruct((t, GDN_W), F32)] * 2,
        scratch_shapes=[pltpu.VMEM((2, heads, GDN_K_DIM, GDN_V_DIM), F32)],
        compiler_params=_params(("parallel", "parallel", "arbitrary")),
        name="gdn_delta_rule",
    )(q, k, v, gates, q, k, v, gates)


def _out_proj_kernel(x_ref, oa_ref, ob_ref, of_ref, obw_ref, z_ref, onorm_ref, w_ref, g_ref, o_ref):
    oc = of_ref[...] + obw_ref[...]
    z = z_ref[...]
    parts = []
    for h in range(GDN_HEADS):
        blk = slice(h * GDN_V_DIM, (h + 1) * GDN_V_DIM)
        parts.append(_rms(oc[:, blk], onorm_ref[...]) * _silu(z[:, blk]))
    og = jnp.concatenate(parts, axis=1).astype(BF16)
    na = oa_ref.shape[1]
    nb = ob_ref.shape[1]
    mix = (jnp.dot(oa_ref[...], w_ref[:na, :], preferred_element_type=F32)
           + jnp.dot(ob_ref[...], w_ref[na:na + nb, :], preferred_element_type=F32)
           + jnp.dot(og, w_ref[na + nb:, :], preferred_element_type=F32))
    o_ref[...] = x_ref[...] + _rms(mix, g_ref[...])


def _out_proj(x, o_a, o_b, o_f, o_bw, z, out_norm, w_out, gain, tm=512):
    t, d = x.shape
    row = lambda n: pl.BlockSpec((tm, n), lambda i: (i, 0))
    return pl.pallas_call(
        _out_proj_kernel,
        grid=(t // tm,),
        in_specs=[row(d), row(o_a.shape[1]), row(o_b.shape[1]), row(GDN_W), row(GDN_W), row(GDN_W),
                  _resident((1, GDN_V_DIM)), _resident(w_out.shape), _resident((1, d))],
        out_specs=row(d),
        out_shape=jax.ShapeDtypeStruct((t, d), F32),
        compiler_params=_params(("parallel",)),
        name="out_proj",
    )(x, o_a, o_b, o_f, o_bw, z, out_norm.reshape(1, -1), w_out, gain.reshape(1, d))


def _ffn_kernel(x_ref, gin_ref, wg_ref, wu_ref, wd_ref, gout_ref, o_ref, hn_ref):
    j = pl.program_id(1)

    @pl.when(j == 0)
    def _():
        hn_ref[...] = _rms(x_ref[...], gin_ref[...]).astype(BF16)
        o_ref[...] = jnp.zeros_like(o_ref)

    hn = hn_ref[...]
    gate = jnp.dot(hn, wg_ref[...], preferred_element_type=F32)
    up = jnp.dot(hn, wu_ref[...], preferred_element_type=F32)
    o_ref[...] += jnp.dot((_silu(gate) * up).astype(BF16), wd_ref[...], preferred_element_type=F32)

    @pl.when(j == pl.num_programs(1) - 1)
    def _():
        o_ref[...] = x_ref[...] + _rms(o_ref[...], gout_ref[...])


def _ffn(x, gain_in, w_gate, w_up, w_down, gain_out, tm=1024, tf=512):
    t, d = x.shape
    f = w_gate.shape[1]
    tm = min(tm, t)
    return pl.pallas_call(
        _ffn_kernel,
        grid=(t // tm, f // tf),
        in_specs=[pl.BlockSpec((tm, d), lambda i, j: (i, 0), pipeline_mode=pl.Buffered(1)),
                  _resident((1, d)),
                  pl.BlockSpec((d, tf), lambda i, j: (0, j)),
                  pl.BlockSpec((d, tf), lambda i, j: (0, j)),
                  pl.BlockSpec((tf, d), lambda i, j: (j, 0)),
                  _resident((1, d))],
        out_specs=pl.BlockSpec((tm, d), lambda i, j: (i, 0)),
        out_shape=jax.ShapeDtypeStruct((t, d), F32),
        scratch_shapes=[pltpu.VMEM((tm, d), BF16)],
        compiler_params=_params(("parallel", "arbitrary")),
        name="swiglu_ffn",
    )(x, gain_in.reshape(1, d), w_gate, w_up, w_down, gain_out.reshape(1, d))


def _rot_half_cols(w):
    half = w.shape[-1] // 2
    return jnp.concatenate([-w[..., half:], w[..., :half]], axis=-1)


def _pad_cols(w, n):
    return jnp.pad(w, [(0, 0)] * (w.ndim - 1) + [(0, n - w.shape[-1])])


IN_WIDTHS = (Q_LORA_RANK, KV_LORA_RANK, 2 * CONV_CHANNELS, 3 * GDN_W, GDN_W, LANES, LANES)


def _layout_w_in(w_in):
    sizes = (Q_LORA_RANK, KV_LORA_RANK, QK_ROPE_DIM, 2 * CONV_CHANNELS, 3 * GDN_W, GDN_W, 4 * GDN_HEADS)
    c_q, c_kv, k_rope, conv, gqkv, gz, gates = jnp.split(w_in, np.cumsum(sizes)[:-1].tolist(), axis=1)
    misc = _pad_cols(jnp.concatenate([k_rope, gates], axis=1), LANES)
    krot = _pad_cols(_rot_half_cols(k_rope), LANES)
    return jnp.concatenate([c_q, c_kv, conv, gqkv, gz, misc, krot], axis=1).astype(BF16)


def _layout_w_uq(w_uq):
    r = w_uq.shape[0]
    w = w_uq.reshape(r, MLA_HEADS, QK_HEAD_DIM)
    nope = w[..., :QK_NOPE_DIM]
    rope = w[..., QK_NOPE_DIM:]
    parts = [nope, _pad_cols(rope, LANES), _pad_cols(_rot_half_cols(rope), LANES)]
    return jnp.concatenate([p.reshape(r, MLA_HEADS * LANES) for p in parts], axis=1).astype(BF16)


def _layout_w_ukv(w_ukv):
    r = w_ukv.shape[0]
    w = w_ukv.reshape(r, MLA_HEADS, QK_NOPE_DIM + V_HEAD_DIM)
    return jnp.concatenate([w[..., :QK_NOPE_DIM].reshape(r, -1), w[..., QK_NOPE_DIM:].reshape(r, -1)],
                           axis=1).astype(BF16)


def _layer(x, cos, sin, batch, seq, pre_mix_norm, w_in, q_a_norm, w_uq, kv_a_norm, w_ukv, conv_dw_w,
           conv_dw_b, conv_ln_g, conv_ln_b, gdn_conv_w, gdn_a_log, gdn_dt_bias, gdn_out_norm, w_out,
           post_mix_norm, pre_ffn_norm, w_gate, w_up, w_down, post_ffn_norm):
    cq, ckv, conv_in, gqkv, gz, misc, krot = _in_proj(x, pre_mix_norm, _layout_w_in(w_in), IN_WIDTHS)
    q_scale = QK_HEAD_DIM ** -0.5 * float(np.log2(np.e))
    q, k, v = _mla_prep(cq, ckv, misc, krot, cos, sin, q_a_norm, kv_a_norm,
                        _layout_w_uq(w_uq), _layout_w_ukv(w_ukv), q_scale)
    o_a = _attention(q, k, v, batch, seq)
    o_b = _conformer(conv_in, conv_dw_w, conv_dw_b, conv_ln_g, conv_ln_b, seq)
    gq, gk, gv, gates = _gdn_prep(gqkv, misc, gdn_conv_w, gdn_a_log, gdn_dt_bias, seq)
    o_f, o_bw = _gdn(gq, gk, gv, gates, batch, seq)
    x = _out_proj(x, o_a, o_b, o_f, o_bw, gz, gdn_out_norm, w_out.astype(BF16), post_mix_norm)
    return _ffn(x, pre_ffn_norm, w_gate.astype(BF16), w_up.astype(BF16), w_down.astype(BF16),
                post_ffn_norm)


def kernel(x, positions, pre_mix_norm, w_in, q_a_norm, w_uq, kv_a_norm, w_ukv, conv_dw_w, conv_dw_b,
           conv_ln_g, conv_ln_b, gdn_conv_w, gdn_a_log, gdn_dt_bias, gdn_out_norm, w_out, post_mix_norm,
           pre_ffn_norm, w_gate, w_up, w_down, post_ffn_norm):
    batch, seq, d = x.shape
    cos, sin = _rope_tables(positions)
    h = x.reshape(batch * seq, d)
    per_layer = (pre_mix_norm, w_in, q_a_norm, w_uq, kv_a_norm, w_ukv, conv_dw_w, conv_dw_b, conv_ln_g,
                 conv_ln_b, gdn_conv_w, gdn_a_log, gdn_dt_bias, gdn_out_norm, w_out, post_mix_norm,
                 pre_ffn_norm, w_gate, w_up, w_down, post_ffn_norm)
    for l in range(pre_mix_norm.shape[0]):
        h = _layer(h, cos, sin, batch, seq, *(p[l] for p in per_layer))
    return h.reshape(batch, seq, d)
```

```python
import functools

import numpy as np
import jax
import jax.numpy as jnp
from jax import lax
from jax.experimental import pallas as pl
from jax.experimental.pallas import tpu as pltpu

F32 = jnp.float32
BF16 = jnp.bfloat16

EPS = 1e-6
LANES = 128
SUBLANES = 8
VMEM_LIMIT = 56 * 1024 * 1024
FFN_VMEM_LIMIT = 62 * 1024 * 1024

MLA_HEADS = 8
Q_LORA_RANK = 512
KV_LORA_RANK = 512
QK_NOPE_DIM = 128
QK_ROPE_DIM = 64
QK_HEAD_DIM = QK_NOPE_DIM + QK_ROPE_DIM
V_HEAD_DIM = 128
ROPE_THETA = 10000.0
QK_PAD_DIM = 2 * LANES
V_PAD_DIM = 2 * LANES
CONV_CHANNELS = 512
CONV_WIDTH = 31
GDN_HEADS = 4
GDN_K_DIM = 128
GDN_V_DIM = 128
GDN_SHORT_CONV = 5
GDN_W = GDN_HEADS * GDN_K_DIM
GDN_CHUNK = 256
GATE_LANE0 = QK_ROPE_DIM


def _params(semantics, vmem_limit=VMEM_LIMIT):
    return pltpu.CompilerParams(dimension_semantics=semantics, vmem_limit_bytes=vmem_limit)


def _resident(shape):
    return pl.BlockSpec(shape, lambda *_: (0,) * len(shape), pipeline_mode=pl.Buffered(1))


def _rms(x, w):
    return x * lax.rsqrt(jnp.mean(x * x, axis=-1, keepdims=True) + EPS) * w


def _sigmoid(x):
    return 1.0 / (1.0 + jnp.exp(-x))


def _silu(x):
    return x * _sigmoid(x)


def _rope_table_kernel(pos_ref, inv_ref, cos_ref, sin_ref):
    ang = pos_ref[...].astype(F32) * inv_ref[...]
    lane = lax.broadcasted_iota(jnp.int32, ang.shape, 1)
    valid = lane < QK_ROPE_DIM
    cos_ref[...] = jnp.where(valid, jnp.cos(ang), 0.0)
    sin_ref[...] = jnp.where(valid, jnp.sin(ang), 0.0)


def _rope_tables(positions):
    t = positions.size
    tm = min(t, 2048)
    half = QK_ROPE_DIM // 2
    inv = 1.0 / (ROPE_THETA ** (jnp.arange(0, QK_ROPE_DIM, 2, dtype=F32) / QK_ROPE_DIM))
    inv_row = jnp.zeros((1, LANES), F32).at[0, :half].set(inv).at[0, half:2 * half].set(inv)
    spec = pl.BlockSpec((tm, LANES), lambda i: (i, 0))
    return pl.pallas_call(
        _rope_table_kernel,
        grid=(t // tm,),
        in_specs=[pl.BlockSpec((tm, 1), lambda i: (i, 0)), _resident((1, LANES))],
        out_specs=[spec, spec],
        out_shape=[jax.ShapeDtypeStruct((t, LANES), F32)] * 2,
        compiler_params=_params(("parallel",)),
        name="rope_tables",
    )(positions.reshape(t, 1), inv_row)


def _in_proj_kernel(x_ref, g_ref, w_ref, *out_refs):
    half = x_ref.shape[0] // 2
    for r in (0, half):
        rows = slice(r, r + half)
        xn = _rms(x_ref[rows, :], g_ref[...]).astype(BF16)
        off = 0
        for o_ref in out_refs:
            n = o_ref.shape[1]
            o_ref[rows, :] = jnp.dot(xn, w_ref[:, off:off + n],
                                     preferred_element_type=F32).astype(o_ref.dtype)
            off += n


def _in_proj(x, gain, w, widths, tm=512):
    t, d = x.shape
    n = w.shape[1]
    assert sum(widths) == n
    return pl.pallas_call(
        _in_proj_kernel,
        grid=(t // tm,),
        in_specs=[pl.BlockSpec((tm, d), lambda i: (i, 0)), _resident((1, d)), _resident((d, n))],
        out_specs=[pl.BlockSpec((tm, wd), lambda i: (i, 0)) for wd in widths],
        out_shape=[jax.ShapeDtypeStruct((t, wd), F32) for wd in widths],
        compiler_params=_params(("parallel",)),
        name="in_proj",
    )(x, gain.reshape(1, d), w)


def _mla_prep_kernel(cq_ref, ckv_ref, misc_ref, krot_ref, cos_ref, sin_ref, qn_ref, kvn_ref,
                     wq_ref, wkv_ref, q_out, k_out, v_out, *, q_scale):
    cos = cos_ref[...]
    sin = sin_ref[...]
    hw = MLA_HEADS * LANES
    cqn = _rms(cq_ref[...], qn_ref[...]).astype(BF16)
    q_nope = jnp.dot(cqn, wq_ref[:, :hw], preferred_element_type=F32)
    q_rope = jnp.dot(cqn, wq_ref[:, hw:2 * hw], preferred_element_type=F32)
    q_rot = jnp.dot(cqn, wq_ref[:, 2 * hw:], preferred_element_type=F32)
    ckvn = _rms(ckv_ref[...], kvn_ref[...]).astype(BF16)
    k_nope = jnp.dot(ckvn, wkv_ref[:, :hw], preferred_element_type=F32)
    v = jnp.dot(ckvn, wkv_ref[:, hw:], preferred_element_type=F32)
    k_rope = (misc_ref[...] * cos + krot_ref[...] * sin).astype(k_out.dtype)
    ones_col = jnp.where(lax.broadcasted_iota(jnp.int32, cos.shape, 1) == 0, 1.0, 0.0).astype(v_out.dtype)
    for h in range(MLA_HEADS):
        v_out[:, h * V_PAD_DIM:h * V_PAD_DIM + V_HEAD_DIM] = (
            v[:, h * V_HEAD_DIM:(h + 1) * V_HEAD_DIM].astype(v_out.dtype))
        v_out[:, h * V_PAD_DIM + V_HEAD_DIM:(h + 1) * V_PAD_DIM] = ones_col
        blk = slice(h * LANES, (h + 1) * LANES)
        lo = slice(h * QK_PAD_DIM, h * QK_PAD_DIM + LANES)
        hi = slice(h * QK_PAD_DIM + LANES, (h + 1) * QK_PAD_DIM)
        q_out[:, lo] = (q_nope[:, blk] * q_scale).astype(q_out.dtype)
        q_out[:, hi] = ((q_rope[:, blk] * cos + q_rot[:, blk] * sin) * q_scale).astype(q_out.dtype)
        k_out[:, lo] = k_nope[:, blk].astype(k_out.dtype)
        k_out[:, hi] = k_rope


def _mla_prep(cq, ckv, misc, krot, cos, sin, q_a_norm, kv_a_norm, wq, wkv, q_scale, tm=512):
    t = cq.shape[0]
    row = lambda n: pl.BlockSpec((tm, n), lambda i: (i, 0))
    qk_w = MLA_HEADS * QK_PAD_DIM
    v_w = MLA_HEADS * V_PAD_DIM
    return pl.pallas_call(
        functools.partial(_mla_prep_kernel, q_scale=q_scale),
        grid=(t // tm,),
        in_specs=[row(Q_LORA_RANK), row(KV_LORA_RANK), row(LANES), row(LANES), row(LANES), row(LANES),
                  _resident((1, Q_LORA_RANK)), _resident((1, KV_LORA_RANK)),
                  _resident(wq.shape), _resident(wkv.shape)],
        out_specs=[row(qk_w), row(qk_w), row(v_w)],
        out_shape=[jax.ShapeDtypeStruct((t, qk_w), BF16), jax.ShapeDtypeStruct((t, qk_w), BF16),
                   jax.ShapeDtypeStruct((t, v_w), BF16)],
        compiler_params=_params(("parallel",)),
        name="mla_prep",
    )(cq, ckv, misc, krot, cos, sin, q_a_norm.reshape(1, -1), kv_a_norm.reshape(1, -1), wq, wkv)


ATTN_TK = 512


def _attn_kernel(q_ref, k_ref, v_ref, o_ref):
    q = q_ref[...]
    tq = q.shape[0]
    m = jnp.full((tq, 1), -jnp.inf, F32)
    acc = jnp.zeros((tq, v_ref.shape[1]), F32)
    for j in range(k_ref.shape[0] // ATTN_TK):
        rows = slice(j * ATTN_TK, (j + 1) * ATTN_TK)
        s = lax.dot_general(q, k_ref[rows, :], (((1,), (1,)), ((), ())), preferred_element_type=F32)
        m_new = jnp.maximum(m, jnp.max(s, axis=-1, keepdims=True))
        p = jnp.exp2(s - m_new).astype(BF16)
        acc = acc * jnp.exp2(m - m_new) + jnp.dot(p, v_ref[rows, :], preferred_element_type=F32)
        m = m_new
    o_ref[...] = (acc[:, :V_HEAD_DIM] / acc[:, V_HEAD_DIM:V_HEAD_DIM + 1]).astype(o_ref.dtype)


def _attention(q, k, v, batch, seq, tq=1024):
    t = q.shape[0]
    nq = seq // tq
    return pl.pallas_call(
        _attn_kernel,
        grid=(batch, MLA_HEADS, nq),
        in_specs=[pl.BlockSpec((tq, QK_PAD_DIM), lambda b, h, i: (b * nq + i, h)),
                  pl.BlockSpec((seq, QK_PAD_DIM), lambda b, h, i: (b, h)),
                  pl.BlockSpec((seq, V_PAD_DIM), lambda b, h, i: (b, h))],
        out_specs=pl.BlockSpec((tq, V_HEAD_DIM), lambda b, h, i: (b * nq + i, h)),
        out_shape=jax.ShapeDtypeStruct((t, MLA_HEADS * V_HEAD_DIM), BF16),
        compiler_params=_params(("parallel", "parallel", "arbitrary")),
        name="mla_attention",
    )(q, k, v)


def _halo_specs(tm, halo, width, n_rows):
    per = tm // halo
    last = n_rows // halo - 1
    return [pl.BlockSpec((tm, width), lambda i: (i, 0)),
            pl.BlockSpec((halo, width), lambda i: (jnp.maximum(i * per - 1, 0), 0)),
            pl.BlockSpec((halo, width), lambda i: (jnp.minimum((i + 1) * per, last), 0))]


def _seq_edges(tiles_per_seq):
    i = pl.program_id(0) % tiles_per_seq
    return i == 0, i == tiles_per_seq - 1


CONF_HALO = 16
CONF_ROWS = 32


def _conformer_kernel(cur_ref, prev_ref, next_ref, w_ref, b_ref, g_ref, beta_ref, o_ref, ext_ref,
                      shift_ref, *, tiles_per_seq):
    tm = cur_ref.shape[0]
    c = CONV_CHANNELS
    first, last = _seq_edges(tiles_per_seq)

    def glu(u):
        return u[:, :c] * _sigmoid(u[:, c:])

    ext_ref[0:CONF_HALO, :] = jnp.where(first, 0.0, glu(prev_ref[...]))
    ext_ref[CONF_HALO:CONF_HALO + tm, :] = glu(cur_ref[...])
    ext_ref[CONF_HALO + tm:, :] = jnp.where(last, 0.0, glu(next_ref[...]))
    n_shift = shift_ref.shape[1]
    for s in range(1, SUBLANES):
        shift_ref[s, :, :] = ext_ref[s:s + n_shift, :]
    pad = CONV_WIDTH // 2
    bias = b_ref[...]
    for r in range(0, tm, CONF_ROWS):
        acc = jnp.broadcast_to(bias, (CONF_ROWS, c))
        for d in range(CONV_WIDTH):
            start = CONF_HALO + r - pad + d
            s = start % SUBLANES
            base = start - s
            win = ext_ref[base:base + CONF_ROWS, :] if s == 0 else shift_ref[s, base:base + CONF_ROWS, :]
            acc = acc + win * w_ref[d:d + 1, :]
        mu = jnp.mean(acc, axis=-1, keepdims=True)
        cen = acc - mu
        var = jnp.mean(cen * cen, axis=-1, keepdims=True)
        y = cen * lax.rsqrt(var + EPS) * g_ref[...] + beta_ref[...]
        o_ref[r:r + CONF_ROWS, :] = _silu(y).astype(o_ref.dtype)


def _conformer(conv_in, dw_w, dw_b, ln_g, ln_b, seq, tm=512):
    t = conv_in.shape[0]
    c = CONV_CHANNELS
    return pl.pallas_call(
        functools.partial(_conformer_kernel, tiles_per_seq=seq // tm),
        grid=(t // tm,),
        in_specs=_halo_specs(tm, CONF_HALO, 2 * c, t) + [
            _resident((CONV_WIDTH, c)), _resident((1, c)), _resident((1, c)), _resident((1, c))],
        out_specs=pl.BlockSpec((tm, c), lambda i: (i, 0)),
        out_shape=jax.ShapeDtypeStruct((t, c), BF16),
        scratch_shapes=[pltpu.VMEM((tm + 2 * CONF_HALO, c), F32),
                        pltpu.VMEM((SUBLANES, tm + 2 * CONF_HALO - SUBLANES, c), F32)],
        compiler_params=_params(("parallel",)),
        name="conformer_conv",
    )(conv_in, conv_in, conv_in, dw_w.reshape(CONV_WIDTH, c), dw_b.reshape(1, c),
      ln_g.reshape(1, c), ln_b.reshape(1, c))


GDN_HALO = 8
GDN_ROWS = 64


def _gdn_prep_kernel(cur_ref, prev_ref, next_ref, misc_ref, w_ref, alog_ref, dtb_ref,
                     q_out, k_out, v_out, gate_out, ext_ref, *, tiles_per_seq):
    tm = cur_ref.shape[0]
    first, last = _seq_edges(tiles_per_seq)
    ext_ref[0:GDN_HALO, :] = jnp.where(first, 0.0, prev_ref[...])
    ext_ref[GDN_HALO:GDN_HALO + tm, :] = cur_ref[...]
    ext_ref[GDN_HALO + tm:, :] = jnp.where(last, 0.0, next_ref[...])
    pad = GDN_SHORT_CONV // 2
    for r in range(0, tm, GDN_ROWS):
        rows = slice(r, r + GDN_ROWS)
        for part, out in enumerate((q_out, k_out, v_out)):
            cols = slice(part * GDN_W, (part + 1) * GDN_W)
            acc = jnp.zeros((GDN_ROWS, GDN_W), F32)
            for d in range(GDN_SHORT_CONV):
                start = GDN_HALO + r - pad + d
                acc = acc + ext_ref[start:start + GDN_ROWS, cols] * w_ref[d:d + 1, cols]
            y = _silu(acc)
            if part == 2:
                out[rows, :] = y
                continue
            scale = GDN_K_DIM ** -0.5 if part == 0 else 1.0
            for h in range(GDN_HEADS):
                blk = slice(h * GDN_K_DIM, (h + 1) * GDN_K_DIM)
                yh = y[:, blk]
                inv = lax.rsqrt(jnp.sum(yh * yh, axis=-1, keepdims=True) + EPS)
                out[rows, blk] = yh * (inv * scale)
    m = misc_ref[...]
    lane = lax.broadcasted_iota(jnp.int32, m.shape, 1) - GATE_LANE0
    is_gate = (lane >= 0) & (lane < 4 * GDN_HEADS)
    is_a = is_gate & ((lane // GDN_HEADS) % 2 == 0)
    z = m + dtb_ref[...]
    softplus = jnp.maximum(z, 0.0) + jnp.log(1.0 + jnp.exp(-jnp.abs(z)))
    g = -jnp.exp(alog_ref[...]) * softplus
    gate_out[...] = jnp.where(is_a, g, jnp.where(is_gate, _sigmoid(m), 0.0))


def _gdn_prep(gqkv, misc, conv_w, a_log, dt_bias, seq, tm=512):
    t = gqkv.shape[0]
    w3 = 3 * GDN_W
    zeros = jnp.zeros((GDN_HEADS,), F32)
    lane_vals = lambda p: jnp.zeros((1, LANES), F32).at[0, GATE_LANE0:GATE_LANE0 + 4 * GDN_HEADS].set(
        jnp.concatenate([p[0], zeros, p[1], zeros]))
    row = lambda n: pl.BlockSpec((tm, n), lambda i: (i, 0))
    return pl.pallas_call(
        functools.partial(_gdn_prep_kernel, tiles_per_seq=seq // tm),
        grid=(t // tm,),
        in_specs=_halo_specs(tm, GDN_HALO, w3, t) + [
            row(LANES), _resident((GDN_SHORT_CONV, w3)), _resident((1, LANES)), _resident((1, LANES))],
        out_specs=[row(GDN_W), row(GDN_W), row(GDN_W), row(LANES)],
        out_shape=[jax.ShapeDtypeStruct((t, GDN_W), F32)] * 3 + [jax.ShapeDtypeStruct((t, LANES), F32)],
        scratch_shapes=[pltpu.VMEM((tm + 2 * GDN_HALO, w3), F32)],
        compiler_params=_params(("parallel",)),
        name="gdn_prep",
    )(gqkv, gqkv, gqkv, misc, conv_w.reshape(GDN_SHORT_CONV, w3), lane_vals(a_log), lane_vals(dt_bias))


def _bdot(a, b):
    return jnp.dot(a.astype(BF16), b.astype(BF16), preferred_element_type=F32)


def _bdot_nt(a, b):
    return lax.dot_general(a.astype(BF16), b.astype(BF16), (((1,), (1,)), ((), ())),
                           preferred_element_type=F32)


def _gdn_kernel(qf_ref, kf_ref, vf_ref, gf_ref, qb_ref, kb_ref, vb_ref, gb_ref, of_ref, ob_ref,
                state_ref, *, heads):
    c = qf_ref.shape[0]
    d = GDN_K_DIM

    @pl.when(pl.program_id(2) == 0)
    def _():
        state_ref[...] = jnp.zeros_like(state_ref)

    ri = lax.broadcasted_iota(jnp.int32, (c, c), 0)
    ci = lax.broadcasted_iota(jnp.int32, (c, c), 1)
    xor = ri ^ ci
    lane_row = lax.broadcasted_iota(jnp.int32, (1, LANES), 1)
    lane_col = lax.broadcasted_iota(jnp.int32, (LANES, 1), 0)

    chains = []
    for direction, (q_ref, k_ref, v_ref, g_ref, o_ref) in enumerate(
            ((qf_ref, kf_ref, vf_ref, gf_ref, of_ref), (qb_ref, kb_ref, vb_ref, gb_ref, ob_ref))):
        lower = direction == 0
        incl = (ri >= ci) if lower else (ri <= ci)
        strict = (ri > ci) if lower else (ri < ci)
        gates = g_ref[...]
        gc_all = jnp.dot(jnp.where(incl, 1.0, 0.0), gates, preferred_element_type=F32,
                         precision=lax.Precision.HIGHEST)
        gc_all_t = gc_all.T
        last = c - 1 if lower else 0
        for h in range(heads):
            blk = slice(h * d, (h + 1) * d)
            g_lane = GATE_LANE0 + 2 * GDN_HEADS * direction + pl.program_id(1) * heads + h
            gc = jnp.sum(jnp.where(lane_row == g_lane, gc_all, 0.0), axis=1, keepdims=True)
            beta = jnp.sum(jnp.where(lane_row == g_lane + GDN_HEADS, gates, 0.0), axis=1, keepdims=True)
            gc_row = jnp.sum(jnp.where(lane_col == g_lane, gc_all_t, 0.0), axis=0, keepdims=True)
            chains.append(dict(q=q_ref[:, blk], k=k_ref[:, blk], v=v_ref[:, blk], gc=gc, beta=beta,
                               gc_row=gc_row, g_tot=gc[last:last + 1, :], incl=incl, strict=strict,
                               o_ref=o_ref, blk=blk, idx=(direction, h)))

    for ch in chains:
        ch["kk"] = _bdot_nt(ch["k"], ch["k"])
    for ch in chains:
        ch["qk"] = _bdot_nt(ch["q"], ch["k"])
    for ch in chains:
        incl = ch["incl"]
        decay = jnp.where(incl, jnp.exp(jnp.where(incl, ch["gc"] - ch["gc_row"], 0.0)), 0.0)
        ch["a"] = jnp.where(ch["strict"], ch["kk"] * ch["beta"] * decay, 0.0)
        ch["qkd"] = (ch["qk"] * decay).astype(BF16)
        ch["t"] = jnp.where(ri == ci, 1.0, 0.0) - jnp.where(xor < 2, ch["a"], 0.0)
    b = 2
    while b < c:
        level = (xor >= b) & (xor < 2 * b)
        for ch in chains:
            ch["x"] = _bdot(jnp.where(level, ch["a"], 0.0), ch["t"])
        for ch in chains:
            ch["t"] = ch["t"] - _bdot(ch["t"], ch["x"])
        b *= 2
    for ch in chains:
        rhs = jnp.concatenate([ch["v"] * ch["beta"], ch["k"] * (ch["beta"] * jnp.exp(ch["gc"]))], axis=1)
        ch["uw"] = _bdot(ch["t"], rhs).astype(BF16)
    for ch in chains:
        ch["z"] = _bdot(ch["qkd"], ch["uw"])
    for ch in chains:
        k_til = ch["k"] * jnp.exp(ch["g_tot"] - ch["gc"])
        ch["y"] = _bdot(k_til.T, ch["uw"])
    for ch in chains:
        state = state_ref[ch["idx"]]
        q_eff = ch["q"] * jnp.exp(ch["gc"]) - ch["z"][:, d:]
        ch["o_ref"][:, ch["blk"]] = _bdot(q_eff, state) + ch["z"][:, :d]
        state_ref[ch["idx"]] = (state * jnp.exp(ch["g_tot"]) - _bdot(ch["y"][:, d:], state)
                                + ch["y"][:, :d])


def _gdn(q, k, v, gates, batch, seq, heads=GDN_HEADS):
    t = q.shape[0]
    c = GDN_CHUNK
    nc = seq // c
    hw = heads * GDN_K_DIM
    gate_f = pl.BlockSpec((c, LANES), lambda b, hp, i: (b * nc + i, 0))
    gate_b = pl.BlockSpec((c, LANES), lambda b, hp, i: (b * nc + nc - 1 - i, 0))
    head_f = pl.BlockSpec((c, hw), lambda b, hp, i: (b * nc + i, hp))
    head_b = pl.BlockSpec((c, hw), lambda b, hp, i: (b * nc + nc - 1 - i, hp))
    return pl.pallas_call(
        functools.partial(_gdn_kernel, heads=heads),
        grid=(batch, GDN_HEADS // heads, nc),
        in_specs=[head_f, head_f, head_f, gate_f, head_b, head_b, head_b, gate_b],
        out_specs=[head_f, head_b],
        out_shape=[jax.ShapeDtypeStruct((t, GDN_W), F32)] * 2,
        scratch_shapes=[pltpu.VMEM((2, heads, GDN_K_DIM, GDN_V_DIM), F32)],
        compiler_params=_params(("parallel", "parallel", "arbitrary")),
        name="gdn_delta_rule",
    )(q, k, v, gates, q, k, v, gates)


def _out_proj_kernel(x_ref, oa_ref, ob_ref, of_ref, obw_ref, z_ref, onorm_ref, w_ref, g_ref, o_ref):
    na = oa_ref.shape[1]
    nb = ob_ref.shape[1]
    half = x_ref.shape[0] // 2
    for r in (0, half):
        rows = slice(r, r + half)
        oc = of_ref[rows, :] + obw_ref[rows, :]
        z = z_ref[rows, :]
        parts = []
        for h in range(GDN_HEADS):
            blk = slice(h * GDN_V_DIM, (h + 1) * GDN_V_DIM)
            parts.append(_rms(oc[:, blk], onorm_ref[...]) * _silu(z[:, blk]))
        og = jnp.concatenate(parts, axis=1).astype(BF16)
        mix = (jnp.dot(oa_ref[rows, :], w_ref[:na, :], preferred_element_type=F32)
               + jnp.dot(ob_ref[rows, :], w_ref[na:na + nb, :], preferred_element_type=F32)
               + jnp.dot(og, w_ref[na + nb:, :], preferred_element_type=F32))
        o_ref[rows, :] = x_ref[rows, :] + _rms(mix, g_ref[...])


def _out_proj(x, o_a, o_b, o_f, o_bw, z, out_norm, w_out, gain, tm=512):
    t, d = x.shape
    row = lambda n: pl.BlockSpec((tm, n), lambda i: (i, 0))
    return pl.pallas_call(
        _out_proj_kernel,
        grid=(t // tm,),
        in_specs=[row(d), row(o_a.shape[1]), row(o_b.shape[1]), row(GDN_W), row(GDN_W), row(GDN_W),
                  _resident((1, GDN_V_DIM)), _resident(w_out.shape), _resident((1, d))],
        out_specs=row(d),
        out_shape=jax.ShapeDtypeStruct((t, d), F32),
        compiler_params=_params(("parallel",)),
        name="out_proj",
    )(x, o_a, o_b, o_f, o_bw, z, out_norm.reshape(1, -1), w_out, gain.reshape(1, d))


def _ffn_kernel(x_ref, gin_ref, wg_ref, wu_ref, wd_ref, gout_ref, o_ref, hn_ref):
    j = pl.program_id(1)

    @pl.when(j == 0)
    def _():
        hn_ref[...] = _rms(x_ref[...], gin_ref[...]).astype(BF16)
        o_ref[...] = jnp.zeros_like(o_ref)

    hn = hn_ref[...]
    gate = jnp.dot(hn, wg_ref[...], preferred_element_type=F32)
    up = jnp.dot(hn, wu_ref[...], preferred_element_type=F32)
    o_ref[...] += jnp.dot((_silu(gate) * up).astype(BF16), wd_ref[...], preferred_element_type=F32)

    @pl.when(j == pl.num_programs(1) - 1)
    def _():
        o_ref[...] = x_ref[...] + _rms(o_ref[...], gout_ref[...])


def _ffn(x, gain_in, w_gate, w_up, w_down, gain_out, tm=1024, tf=512):
    t, d = x.shape
    f = w_gate.shape[1]
    tm = min(tm, t)
    return pl.pallas_call(
        _ffn_kernel,
        grid=(t // tm, f // tf),
        in_specs=[pl.BlockSpec((tm, d), lambda i, j: (i, 0)),
                  _resident((1, d)),
                  pl.BlockSpec((d, tf), lambda i, j: (0, j)),
                  pl.BlockSpec((d, tf), lambda i, j: (0, j)),
                  pl.BlockSpec((tf, d), lambda i, j: (j, 0)),
                  _resident((1, d))],
        out_specs=pl.BlockSpec((tm, d), lambda i, j: (i, 0)),
        out_shape=jax.ShapeDtypeStruct((t, d), F32),
        scratch_shapes=[pltpu.VMEM((tm, d), BF16)],
        compiler_params=_params(("parallel", "arbitrary"), FFN_VMEM_LIMIT),
        name="swiglu_ffn",
    )(x, gain_in.reshape(1, d), w_gate, w_up, w_down, gain_out.reshape(1, d))


def _rot_half_cols(w):
    half = w.shape[-1] // 2
    return jnp.concatenate([-w[..., half:], w[..., :half]], axis=-1)


def _pad_cols(w, n):
    return jnp.pad(w, [(0, 0)] * (w.ndim - 1) + [(0, n - w.shape[-1])])


IN_WIDTHS = (Q_LORA_RANK, KV_LORA_RANK, 2 * CONV_CHANNELS, 3 * GDN_W, GDN_W, LANES, LANES)


def _layout_w_in(w_in):
    sizes = (Q_LORA_RANK, KV_LORA_RANK, QK_ROPE_DIM, 2 * CONV_CHANNELS, 3 * GDN_W, GDN_W, 4 * GDN_HEADS)
    c_q, c_kv, k_rope, conv, gqkv, gz, gates = jnp.split(w_in, np.cumsum(sizes)[:-1].tolist(), axis=1)
    misc = _pad_cols(jnp.concatenate([k_rope, gates], axis=1), LANES)
    krot = _pad_cols(_rot_half_cols(k_rope), LANES)
    return jnp.concatenate([c_q, c_kv, conv, gqkv, gz, misc, krot], axis=1).astype(BF16)


def _layout_w_uq(w_uq):
    r = w_uq.shape[0]
    w = w_uq.reshape(r, MLA_HEADS, QK_HEAD_DIM)
    nope = w[..., :QK_NOPE_DIM]
    rope = w[..., QK_NOPE_DIM:]
    parts = [nope, _pad_cols(rope, LANES), _pad_cols(_rot_half_cols(rope), LANES)]
    return jnp.concatenate([p.reshape(r, MLA_HEADS * LANES) for p in parts], axis=1).astype(BF16)


def _layout_w_ukv(w_ukv):
    r = w_ukv.shape[0]
    w = w_ukv.reshape(r, MLA_HEADS, QK_NOPE_DIM + V_HEAD_DIM)
    return jnp.concatenate([w[..., :QK_NOPE_DIM].reshape(r, -1), w[..., QK_NOPE_DIM:].reshape(r, -1)],
                           axis=1).astype(BF16)


def _layer(x, cos, sin, batch, seq, pre_mix_norm, w_in, q_a_norm, w_uq, kv_a_norm, w_ukv, conv_dw_w,
           conv_dw_b, conv_ln_g, conv_ln_b, gdn_conv_w, gdn_a_log, gdn_dt_bias, gdn_out_norm, w_out,
           post_mix_norm, pre_ffn_norm, w_gate, w_up, w_down, post_ffn_norm):
    cq, ckv, conv_in, gqkv, gz, misc, krot = _in_proj(x, pre_mix_norm, _layout_w_in(w_in), IN_WIDTHS)
    q_scale = QK_HEAD_DIM ** -0.5 * float(np.log2(np.e))
    q, k, v = _mla_prep(cq, ckv, misc, krot, cos, sin, q_a_norm, kv_a_norm,
                        _layout_w_uq(w_uq), _layout_w_ukv(w_ukv), q_scale)
    o_a = _attention(q, k, v, batch, seq)
    o_b = _conformer(conv_in, conv_dw_w, conv_dw_b, conv_ln_g, conv_ln_b, seq)
    gq, gk, gv, gates = _gdn_prep(gqkv, misc, gdn_conv_w, gdn_a_log, gdn_dt_bias, seq)
    o_f, o_bw = _gdn(gq, gk, gv, gates, batch, seq)
    x = _out_proj(x, o_a, o_b, o_f, o_bw, gz, gdn_out_norm, w_out.astype(BF16), post_mix_norm)
    return _ffn(x, pre_ffn_norm, w_gate.astype(BF16), w_up.astype(BF16), w_down.astype(BF16),
                post_ffn_norm)


def kernel(x, positions, pre_mix_norm, w_in, q_a_norm, w_uq, kv_a_norm, w_ukv, conv_dw_w, conv_dw_b,
           conv_ln_g, conv_ln_b, gdn_conv_w, gdn_a_log, gdn_dt_bias, gdn_out_norm, w_out, post_mix_norm,
           pre_ffn_norm, w_gate, w_up, w_down, post_ffn_norm):
    batch, seq, d = x.shape
    cos, sin = _rope_tables(positions)
    h = x.reshape(batch * seq, d)
    per_layer = (pre_mix_norm, w_in, q_a_norm, w_uq, kv_a_norm, w_ukv, conv_dw_w, conv_dw_b, conv_ln_g,
                 conv_ln_b, gdn_conv_w, gdn_a_log, gdn_dt_bias, gdn_out_norm, w_out, post_mix_norm,
                 pre_ffn_norm, w_gate, w_up, w_down, post_ffn_norm)
    for l in range(pre_mix_norm.shape[0]):
        h = _layer(h, cos, sin, batch, seq, *(p[l] for p in per_layer))
    return h.reshape(batch, seq, d)
```

```python
import functools

import numpy as np
import jax
import jax.numpy as jnp
from jax import lax
from jax.experimental import pallas as pl
from jax.experimental.pallas import tpu as pltpu

F32 = jnp.float32
BF16 = jnp.bfloat16

EPS = 1e-6
LANES = 128
SUBLANES = 8
VMEM_LIMIT = 56 * 1024 * 1024
FFN_VMEM_LIMIT = 62 * 1024 * 1024

MLA_HEADS = 8
Q_LORA_RANK = 512
KV_LORA_RANK = 512
QK_NOPE_DIM = 128
QK_ROPE_DIM = 64
QK_HEAD_DIM = QK_NOPE_DIM + QK_ROPE_DIM
V_HEAD_DIM = 128
ROPE_THETA = 10000.0
QK_PAD_DIM = 2 * LANES
V_PAD_DIM = 2 * LANES
CONV_CHANNELS = 512
CONV_WIDTH = 31
GDN_HEADS = 4
GDN_K_DIM = 128
GDN_V_DIM = 128
GDN_SHORT_CONV = 5
GDN_W = GDN_HEADS * GDN_K_DIM
GDN_CHUNK = 256
GATE_LANE0 = QK_ROPE_DIM


def _params(semantics, vmem_limit=VMEM_LIMIT):
    return pltpu.CompilerParams(dimension_semantics=semantics, vmem_limit_bytes=vmem_limit)


def _resident(shape):
    return pl.BlockSpec(shape, lambda *_: (0,) * len(shape), pipeline_mode=pl.Buffered(1))


def _rms(x, w):
    return x * lax.rsqrt(jnp.mean(x * x, axis=-1, keepdims=True) + EPS) * w


def _sigmoid(x):
    return 1.0 / (1.0 + jnp.exp(-x))


def _silu(x):
    return x * _sigmoid(x)


def _rope_table_kernel(pos_ref, inv_ref, cos_ref, sin_ref):
    ang = pos_ref[...].astype(F32) * inv_ref[...]
    lane = lax.broadcasted_iota(jnp.int32, ang.shape, 1)
    valid = lane < QK_ROPE_DIM
    cos_ref[...] = jnp.where(valid, jnp.cos(ang), 0.0)
    sin_ref[...] = jnp.where(valid, jnp.sin(ang), 0.0)


def _rope_tables(positions):
    t = positions.size
    tm = min(t, 2048)
    half = QK_ROPE_DIM // 2
    inv = 1.0 / (ROPE_THETA ** (jnp.arange(0, QK_ROPE_DIM, 2, dtype=F32) / QK_ROPE_DIM))
    inv_row = jnp.zeros((1, LANES), F32).at[0, :half].set(inv).at[0, half:2 * half].set(inv)
    spec = pl.BlockSpec((tm, LANES), lambda i: (i, 0))
    return pl.pallas_call(
        _rope_table_kernel,
        grid=(t // tm,),
        in_specs=[pl.BlockSpec((tm, 1), lambda i: (i, 0)), _resident((1, LANES))],
        out_specs=[spec, spec],
        out_shape=[jax.ShapeDtypeStruct((t, LANES), F32)] * 2,
        compiler_params=_params(("parallel",)),
        name="rope_tables",
    )(positions.reshape(t, 1), inv_row)


def _in_proj_kernel(x_ref, g_ref, w_ref, *out_refs):
    half = x_ref.shape[0] // 2
    for r in (0, half):
        rows = slice(r, r + half)
        xn = _rms(x_ref[rows, :], g_ref[...]).astype(BF16)
        off = 0
        for o_ref in out_refs:
            n = o_ref.shape[1]
            o_ref[rows, :] = jnp.dot(xn, w_ref[:, off:off + n],
                                     preferred_element_type=F32).astype(o_ref.dtype)
            off += n


def _in_proj(x, gain, w, widths, tm=512):
    t, d = x.shape
    n = w.shape[1]
    assert sum(widths) == n
    return pl.pallas_call(
        _in_proj_kernel,
        grid=(t // tm,),
        in_specs=[pl.BlockSpec((tm, d), lambda i: (i, 0)), _resident((1, d)), _resident((d, n))],
        out_specs=[pl.BlockSpec((tm, wd), lambda i: (i, 0)) for wd in widths],
        out_shape=[jax.ShapeDtypeStruct((t, wd), F32) for wd in widths],
        compiler_params=_params(("parallel",)),
        name="in_proj",
    )(x, gain.reshape(1, d), w)


def _mla_prep_kernel(cq_ref, ckv_ref, misc_ref, krot_ref, cos_ref, sin_ref, qn_ref, kvn_ref,
                     wq_ref, wkv_ref, q_out, k_out, v_out, *, q_scale):
    cos = cos_ref[...]
    sin = sin_ref[...]
    hw = MLA_HEADS * LANES
    cqn = _rms(cq_ref[...], qn_ref[...]).astype(BF16)
    q_nope = jnp.dot(cqn, wq_ref[:, :hw], preferred_element_type=F32)
    q_rope = jnp.dot(cqn, wq_ref[:, hw:2 * hw], preferred_element_type=F32)
    q_rot = jnp.dot(cqn, wq_ref[:, 2 * hw:], preferred_element_type=F32)
    ckvn = _rms(ckv_ref[...], kvn_ref[...]).astype(BF16)
    k_nope = jnp.dot(ckvn, wkv_ref[:, :hw], preferred_element_type=F32)
    v = jnp.dot(ckvn, wkv_ref[:, hw:], preferred_element_type=F32)
    k_rope = (misc_ref[...] * cos + krot_ref[...] * sin).astype(k_out.dtype)
    ones_col = jnp.where(lax.broadcasted_iota(jnp.int32, cos.shape, 1) == 0, 1.0, 0.0).astype(v_out.dtype)
    for h in range(MLA_HEADS):
        v_out[:, h * V_PAD_DIM:h * V_PAD_DIM + V_HEAD_DIM] = (
            v[:, h * V_HEAD_DIM:(h + 1) * V_HEAD_DIM].astype(v_out.dtype))
        v_out[:, h * V_PAD_DIM + V_HEAD_DIM:(h + 1) * V_PAD_DIM] = ones_col
        blk = slice(h * LANES, (h + 1) * LANES)
        lo = slice(h * QK_PAD_DIM, h * QK_PAD_DIM + LANES)
        hi = slice(h * QK_PAD_DIM + LANES, (h + 1) * QK_PAD_DIM)
        q_out[:, lo] = (q_nope[:, blk] * q_scale).astype(q_out.dtype)
        q_out[:, hi] = ((q_rope[:, blk] * cos + q_rot[:, blk] * sin) * q_scale).astype(q_out.dtype)
        k_out[:, lo] = k_nope[:, blk].astype(k_out.dtype)
        k_out[:, hi] = k_rope


def _mla_prep(cq, ckv, misc, krot, cos, sin, q_a_norm, kv_a_norm, wq, wkv, q_scale, tm=512):
    t = cq.shape[0]
    row = lambda n: pl.BlockSpec((tm, n), lambda i: (i, 0))
    qk_w = MLA_HEADS * QK_PAD_DIM
    v_w = MLA_HEADS * V_PAD_DIM
    return pl.pallas_call(
        functools.partial(_mla_prep_kernel, q_scale=q_scale),
        grid=(t // tm,),
        in_specs=[row(Q_LORA_RANK), row(KV_LORA_RANK), row(LANES), row(LANES), row(LANES), row(LANES),
                  _resident((1, Q_LORA_RANK)), _resident((1, KV_LORA_RANK)),
                  _resident(wq.shape), _resident(wkv.shape)],
        out_specs=[row(qk_w), row(qk_w), row(v_w)],
        out_shape=[jax.ShapeDtypeStruct((t, qk_w), BF16), jax.ShapeDtypeStruct((t, qk_w), BF16),
                   jax.ShapeDtypeStruct((t, v_w), BF16)],
        compiler_params=_params(("parallel",)),
        name="mla_prep",
    )(cq, ckv, misc, krot, cos, sin, q_a_norm.reshape(1, -1), kv_a_norm.reshape(1, -1), wq, wkv)


ATTN_TK = 512


def _attn_kernel(q_ref, k_ref, v_ref, o_ref, *, heads):
    tq = q_ref.shape[0]
    m = [jnp.full((tq, 1), -jnp.inf, F32)] * heads
    acc = [jnp.zeros((tq, V_PAD_DIM), F32)] * heads
    for j in range(k_ref.shape[0] // ATTN_TK):
        rows = slice(j * ATTN_TK, (j + 1) * ATTN_TK)
        for h in range(heads):
            qk_cols = slice(h * QK_PAD_DIM, (h + 1) * QK_PAD_DIM)
            v_cols = slice(h * V_PAD_DIM, (h + 1) * V_PAD_DIM)
            s = lax.dot_general(q_ref[:, qk_cols], k_ref[rows, qk_cols], (((1,), (1,)), ((), ())),
                                preferred_element_type=F32)
            m_new = jnp.maximum(m[h], jnp.max(s, axis=-1, keepdims=True))
            p = jnp.exp2(s - m_new).astype(BF16)
            acc[h] = acc[h] * jnp.exp2(m[h] - m_new) + jnp.dot(p, v_ref[rows, v_cols],
                                                               preferred_element_type=F32)
            m[h] = m_new
    for h in range(heads):
        o_ref[:, h * V_HEAD_DIM:(h + 1) * V_HEAD_DIM] = (
            acc[h][:, :V_HEAD_DIM] / acc[h][:, V_HEAD_DIM:V_HEAD_DIM + 1]).astype(o_ref.dtype)


def _attention(q, k, v, batch, seq, tq=1024, heads=2):
    t = q.shape[0]
    nq = seq // tq
    return pl.pallas_call(
        functools.partial(_attn_kernel, heads=heads),
        grid=(batch, MLA_HEADS // heads, nq),
        in_specs=[pl.BlockSpec((tq, heads * QK_PAD_DIM), lambda b, h, i: (b * nq + i, h)),
                  pl.BlockSpec((seq, heads * QK_PAD_DIM), lambda b, h, i: (b, h)),
                  pl.BlockSpec((seq, heads * V_PAD_DIM), lambda b, h, i: (b, h))],
        out_specs=pl.BlockSpec((tq, heads * V_HEAD_DIM), lambda b, h, i: (b * nq + i, h)),
        out_shape=jax.ShapeDtypeStruct((t, MLA_HEADS * V_HEAD_DIM), BF16),
        compiler_params=_params(("parallel", "parallel", "arbitrary")),
        name="mla_attention",
    )(q, k, v)


def _halo_specs(tm, halo, width, n_rows):
    per = tm // halo
    last = n_rows // halo - 1
    return [pl.BlockSpec((tm, width), lambda i: (i, 0)),
            pl.BlockSpec((halo, width), lambda i: (jnp.maximum(i * per - 1, 0), 0)),
            pl.BlockSpec((halo, width), lambda i: (jnp.minimum((i + 1) * per, last), 0))]


def _seq_edges(tiles_per_seq):
    i = pl.program_id(0) % tiles_per_seq
    return i == 0, i == tiles_per_seq - 1


CONF_HALO = 16
CONF_ROWS = 32


def _conformer_kernel(cur_ref, prev_ref, next_ref, w_ref, b_ref, g_ref, beta_ref, o_ref, ext_ref,
                      shift_ref, *, tiles_per_seq):
    tm = cur_ref.shape[0]
    c = CONV_CHANNELS
    first, last = _seq_edges(tiles_per_seq)

    def glu(u):
        return u[:, :c] * _sigmoid(u[:, c:])

    ext_ref[0:CONF_HALO, :] = jnp.where(first, 0.0, glu(prev_ref[...]))
    ext_ref[CONF_HALO:CONF_HALO + tm, :] = glu(cur_ref[...])
    ext_ref[CONF_HALO + tm:, :] = jnp.where(last, 0.0, glu(next_ref[...]))
    n_shift = shift_ref.shape[1]
    for s in range(1, SUBLANES):
        shift_ref[s, :, :] = ext_ref[s:s + n_shift, :]
    pad = CONV_WIDTH // 2
    bias = b_ref[...]
    for r in range(0, tm, CONF_ROWS):
        acc = jnp.broadcast_to(bias, (CONF_ROWS, c))
        for d in range(CONV_WIDTH):
            start = CONF_HALO + r - pad + d
            s = start % SUBLANES
            base = start - s
            win = ext_ref[base:base + CONF_ROWS, :] if s == 0 else shift_ref[s, base:base + CONF_ROWS, :]
            acc = acc + win * w_ref[d:d + 1, :]
        mu = jnp.mean(acc, axis=-1, keepdims=True)
        cen = acc - mu
        var = jnp.mean(cen * cen, axis=-1, keepdims=True)
        y = cen * lax.rsqrt(var + EPS) * g_ref[...] + beta_ref[...]
        o_ref[r:r + CONF_ROWS, :] = _silu(y).astype(o_ref.dtype)


def _conformer(conv_in, dw_w, dw_b, ln_g, ln_b, seq, tm=512):
    t = conv_in.shape[0]
    c = CONV_CHANNELS
    return pl.pallas_call(
        functools.partial(_conformer_kernel, tiles_per_seq=seq // tm),
        grid=(t // tm,),
        in_specs=_halo_specs(tm, CONF_HALO, 2 * c, t) + [
            _resident((CONV_WIDTH, c)), _resident((1, c)), _resident((1, c)), _resident((1, c))],
        out_specs=pl.BlockSpec((tm, c), lambda i: (i, 0)),
        out_shape=jax.ShapeDtypeStruct((t, c), BF16),
        scratch_shapes=[pltpu.VMEM((tm + 2 * CONF_HALO, c), F32),
                        pltpu.VMEM((SUBLANES, tm + 2 * CONF_HALO - SUBLANES, c), F32)],
        compiler_params=_params(("parallel",)),
        name="conformer_conv",
    )(conv_in, conv_in, conv_in, dw_w.reshape(CONV_WIDTH, c), dw_b.reshape(1, c),
      ln_g.reshape(1, c), ln_b.reshape(1, c))


GDN_HALO = 8
GDN_ROWS = 64


def _gdn_prep_kernel(cur_ref, prev_ref, next_ref, misc_ref, w_ref, alog_ref, dtb_ref,
                     q_out, k_out, v_out, gate_out, ext_ref, *, tiles_per_seq):
    tm = cur_ref.shape[0]
    first, last = _seq_edges(tiles_per_seq)
    ext_ref[0:GDN_HALO, :] = jnp.where(first, 0.0, prev_ref[...])
    ext_ref[GDN_HALO:GDN_HALO + tm, :] = cur_ref[...]
    ext_ref[GDN_HALO + tm:, :] = jnp.where(last, 0.0, next_ref[...])
    pad = GDN_SHORT_CONV // 2
    for r in range(0, tm, GDN_ROWS):
        rows = slice(r, r + GDN_ROWS)
        for part, out in enumerate((q_out, k_out, v_out)):
            cols = slice(part * GDN_W, (part + 1) * GDN_W)
            acc = jnp.zeros((GDN_ROWS, GDN_W), F32)
            for d in range(GDN_SHORT_CONV):
                start = GDN_HALO + r - pad + d
                acc = acc + ext_ref[start:start + GDN_ROWS, cols] * w_ref[d:d + 1, cols]
            y = _silu(acc)
            if part == 2:
                out[rows, :] = y
                continue
            scale = GDN_K_DIM ** -0.5 if part == 0 else 1.0
            for h in range(GDN_HEADS):
                blk = slice(h * GDN_K_DIM, (h + 1) * GDN_K_DIM)
                yh = y[:, blk]
                inv = lax.rsqrt(jnp.sum(yh * yh, axis=-1, keepdims=True) + EPS)
                out[rows, blk] = yh * (inv * scale)
    m = misc_ref[...]
    lane = lax.broadcasted_iota(jnp.int32, m.shape, 1) - GATE_LANE0
    is_gate = (lane >= 0) & (lane < 4 * GDN_HEADS)
    is_a = is_gate & ((lane // GDN_HEADS) % 2 == 0)
    z = m + dtb_ref[...]
    softplus = jnp.maximum(z, 0.0) + jnp.log(1.0 + jnp.exp(-jnp.abs(z)))
    g = -jnp.exp(alog_ref[...]) * softplus
    gate_out[...] = jnp.where(is_a, g, jnp.where(is_gate, _sigmoid(m), 0.0))


def _gdn_prep(gqkv, misc, conv_w, a_log, dt_bias, seq, tm=512):
    t = gqkv.shape[0]
    w3 = 3 * GDN_W
    zeros = jnp.zeros((GDN_HEADS,), F32)
    lane_vals = lambda p: jnp.zeros((1, LANES), F32).at[0, GATE_LANE0:GATE_LANE0 + 4 * GDN_HEADS].set(
        jnp.concatenate([p[0], zeros, p[1], zeros]))
    row = lambda n: pl.BlockSpec((tm, n), lambda i: (i, 0))
    return pl.pallas_call(
        functools.partial(_gdn_prep_kernel, tiles_per_seq=seq // tm),
        grid=(t // tm,),
        in_specs=_halo_specs(tm, GDN_HALO, w3, t) + [
            row(LANES), _resident((GDN_SHORT_CONV, w3)), _resident((1, LANES)), _resident((1, LANES))],
        out_specs=[row(GDN_W), row(GDN_W), row(GDN_W), row(LANES)],
        out_shape=[jax.ShapeDtypeStruct((t, GDN_W), F32)] * 3 + [jax.ShapeDtypeStruct((t, LANES), F32)],
        scratch_shapes=[pltpu.VMEM((tm + 2 * GDN_HALO, w3), F32)],
        compiler_params=_params(("parallel",)),
        name="gdn_prep",
    )(gqkv, gqkv, gqkv, misc, conv_w.reshape(GDN_SHORT_CONV, w3), lane_vals(a_log), lane_vals(dt_bias))


def _bdot(a, b):
    return jnp.dot(a.astype(BF16), b.astype(BF16), preferred_element_type=F32)


def _bdot_nt(a, b):
    return lax.dot_general(a.astype(BF16), b.astype(BF16), (((1,), (1,)), ((), ())),
                           preferred_element_type=F32)


def _gdn_kernel(qf_ref, kf_ref, vf_ref, gf_ref, qb_ref, kb_ref, vb_ref, gb_ref, of_ref, ob_ref,
                state_ref, *, heads):
    c = qf_ref.shape[0]
    d = GDN_K_DIM

    @pl.when(pl.program_id(2) == 0)
    def _():
        state_ref[...] = jnp.zeros_like(state_ref)

    ri = lax.broadcasted_iota(jnp.int32, (c, c), 0)
    ci = lax.broadcasted_iota(jnp.int32, (c, c), 1)
    xor = ri ^ ci
    lane_row = lax.broadcasted_iota(jnp.int32, (1, LANES), 1)
    lane_col = lax.broadcasted_iota(jnp.int32, (LANES, 1), 0)

    chains = []
    for direction, (q_ref, k_ref, v_ref, g_ref, o_ref) in enumerate(
            ((qf_ref, kf_ref, vf_ref, gf_ref, of_ref), (qb_ref, kb_ref, vb_ref, gb_ref, ob_ref))):
        lower = direction == 0
        incl = (ri >= ci) if lower else (ri <= ci)
        strict = (ri > ci) if lower else (ri < ci)
        gates = g_ref[...]
        gc_all = jnp.dot(jnp.where(incl, 1.0, 0.0), gates, preferred_element_type=F32,
                         precision=lax.Precision.HIGHEST)
        gc_all_t = gc_all.T
        last = c - 1 if lower else 0
        for h in range(heads):
            blk = slice(h * d, (h + 1) * d)
            g_lane = GATE_LANE0 + 2 * GDN_HEADS * direction + pl.program_id(1) * heads + h
            gc = jnp.sum(jnp.where(lane_row == g_lane, gc_all, 0.0), axis=1, keepdims=True)
            beta = jnp.sum(jnp.where(lane_row == g_lane + GDN_HEADS, gates, 0.0), axis=1, keepdims=True)
            gc_row = jnp.sum(jnp.where(lane_col == g_lane, gc_all_t, 0.0), axis=0, keepdims=True)
            chains.append(dict(q=q_ref[:, blk], k=k_ref[:, blk], v=v_ref[:, blk], gc=gc, beta=beta,
                               gc_row=gc_row, g_tot=gc[last:last + 1, :], incl=incl, strict=strict,
                               o_ref=o_ref, blk=blk, idx=(direction, h)))

    for ch in chains:
        ch["kk"] = _bdot_nt(ch["k"], ch["k"])
    for ch in chains:
        ch["qk"] = _bdot_nt(ch["q"], ch["k"])
    for ch in chains:
        incl = ch["incl"]
        decay = jnp.where(incl, jnp.exp(jnp.where(incl, ch["gc"] - ch["gc_row"], 0.0)), 0.0)
        ch["a"] = jnp.where(ch["strict"], ch["kk"] * ch["beta"] * decay, 0.0)
        ch["qkd"] = (ch["qk"] * decay).astype(BF16)
        ch["t"] = jnp.where(ri == ci, 1.0, 0.0) - jnp.where(xor < 2, ch["a"], 0.0)
    b = 2
    while b < c:
        level = (xor >= b) & (xor < 2 * b)
        for ch in chains:
            ch["x"] = _bdot(jnp.where(level, ch["a"], 0.0), ch["t"])
        for ch in chains:
            ch["t"] = ch["t"] - _bdot(ch["t"], ch["x"])
        b *= 2
    for ch in chains:
        rhs = jnp.concatenate([ch["v"] * ch["beta"], ch["k"] * (ch["beta"] * jnp.exp(ch["gc"]))], axis=1)
        ch["uw"] = _bdot(ch["t"], rhs).astype(BF16)
    for ch in chains:
        ch["z"] = _bdot(ch["qkd"], ch["uw"])
    for ch in chains:
        k_til = ch["k"] * jnp.exp(ch["g_tot"] - ch["gc"])
        ch["y"] = _bdot(k_til.T, ch["uw"])
    for ch in chains:
        state = state_ref[ch["idx"]]
        q_eff = ch["q"] * jnp.exp(ch["gc"]) - ch["z"][:, d:]
        ch["o_ref"][:, ch["blk"]] = _bdot(q_eff, state) + ch["z"][:, :d]
        state_ref[ch["idx"]] = (state * jnp.exp(ch["g_tot"]) - _bdot(ch["y"][:, d:], state)
                                + ch["y"][:, :d])


def _gdn(q, k, v, gates, batch, seq, heads=GDN_HEADS):
    t = q.shape[0]
    c = GDN_CHUNK
    nc = seq // c
    hw = heads * GDN_K_DIM
    gate_f = pl.BlockSpec((c, LANES), lambda b, hp, i: (b * nc + i, 0))
    gate_b = pl.BlockSpec((c, LANES), lambda b, hp, i: (b * nc + nc - 1 - i, 0))
    head_f = pl.BlockSpec((c, hw), lambda b, hp, i: (b * nc + i, hp))
    head_b = pl.BlockSpec((c, hw), lambda b, hp, i: (b * nc + nc - 1 - i, hp))
    return pl.pallas_call(
        functools.partial(_gdn_kernel, heads=heads),
        grid=(batch, GDN_HEADS // heads, nc),
        in_specs=[head_f, head_f, head_f, gate_f, head_b, head_b, head_b, gate_b],
        out_specs=[head_f, head_b],
        out_shape=[jax.ShapeDtypeStruct((t, GDN_W), F32)] * 2,
        scratch_shapes=[pltpu.VMEM((2, heads, GDN_K_DIM, GDN_V_DIM), F32)],
        compiler_params=_params(("parallel", "parallel", "arbitrary")),
        name="gdn_delta_rule",
    )(q, k, v, gates, q, k, v, gates)


def _out_proj_kernel(x_ref, oa_ref, ob_ref, of_ref, obw_ref, z_ref, onorm_ref, w_ref, g_ref, o_ref):
    na = oa_ref.shape[1]
    nb = ob_ref.shape[1]
    half = x_ref.shape[0] // 2
    for r in (0, half):
        rows = slice(r, r + half)
        oc = of_ref[rows, :] + obw_ref[rows, :]
        z = z_ref[rows, :]
        parts = []
        for h in range(GDN_HEADS):
            blk = slice(h * GDN_V_DIM, (h + 1) * GDN_V_DIM)
            parts.append(_rms(oc[:, blk], onorm_ref[...]) * _silu(z[:, blk]))
        og = jnp.concatenate(parts, axis=1).astype(BF16)
        mix = (jnp.dot(oa_ref[rows, :], w_ref[:na, :], preferred_element_type=F32)
               + jnp.dot(ob_ref[rows, :], w_ref[na:na + nb, :], preferred_element_type=F32)
               + jnp.dot(og, w_ref[na + nb:, :], preferred_element_type=F32))
        o_ref[rows, :] = x_ref[rows, :] + _rms(mix, g_ref[...])


def _out_proj(x, o_a, o_b, o_f, o_bw, z, out_norm, w_out, gain, tm=512):
    t, d = x.shape
    row = lambda n: pl.BlockSpec((tm, n), lambda i: (i, 0))
    return pl.pallas_call(
        _out_proj_kernel,
        grid=(t // tm,),
        in_specs=[row(d), row(o_a.shape[1]), row(o_b.shape[1]), row(GDN_W), row(GDN_W), row(GDN_W),
                  _resident((1, GDN_V_DIM)), _resident(w_out.shape), _resident((1, d))],
        out_specs=row(d),
        out_shape=jax.ShapeDtypeStruct((t, d), F32),
        compiler_params=_params(("parallel",)),
        name="out_proj",
    )(x, o_a, o_b, o_f, o_bw, z, out_norm.reshape(1, -1), w_out, gain.reshape(1, d))


def _ffn_kernel(x_ref, gin_ref, wg_ref, wu_ref, wd_ref, gout_ref, o_ref, hn_ref):
    j = pl.program_id(1)

    @pl.when(j == 0)
    def _():
        hn_ref[...] = _rms(x_ref[...], gin_ref[...]).astype(BF16)
        o_ref[...] = jnp.zeros_like(o_ref)

    hn = hn_ref[...]
    gate = jnp.dot(hn, wg_ref[...], preferred_element_type=F32)
    up = jnp.dot(hn, wu_ref[...], preferred_element_type=F32)
    o_ref[...] += jnp.dot((_silu(gate) * up).astype(BF16), wd_ref[...], preferred_element_type=F32)

    @pl.when(j == pl.num_programs(1) - 1)
    def _():
        o_ref[...] = x_ref[...] + _rms(o_ref[...], gout_ref[...])


def _ffn(x, gain_in, w_gate, w_up, w_down, gain_out, tm=1024, tf=512):
    t, d = x.shape
    f = w_gate.shape[1]
    tm = min(tm, t)
    return pl.pallas_call(
        _ffn_kernel,
        grid=(t // tm, f // tf),
        in_specs=[pl.BlockSpec((tm, d), lambda i, j: (i, 0)),
                  _resident((1, d)),
                  pl.BlockSpec((d, tf), lambda i, j: (0, j)),
                  pl.BlockSpec((d, tf), lambda i, j: (0, j)),
                  pl.BlockSpec((tf, d), lambda i, j: (j, 0)),
                  _resident((1, d))],
        out_specs=pl.BlockSpec((tm, d), lambda i, j: (i, 0)),
        out_shape=jax.ShapeDtypeStruct((t, d), F32),
        scratch_shapes=[pltpu.VMEM((tm, d), BF16)],
        compiler_params=_params(("parallel", "arbitrary"), FFN_VMEM_LIMIT),
        name="swiglu_ffn",
    )(x, gain_in.reshape(1, d), w_gate, w_up, w_down, gain_out.reshape(1, d))


def _cast_kernel(x_ref, o_ref):
    o_ref[...] = x_ref[...].astype(o_ref.dtype)


def _to_bf16(w, rows=256):
    r, c = w.shape
    spec = pl.BlockSpec((rows, c), lambda i: (i, 0))
    return pl.pallas_call(
        _cast_kernel, grid=(r // rows,), in_specs=[spec], out_specs=spec,
        out_shape=jax.ShapeDtypeStruct((r, c), BF16),
        compiler_params=_params(("parallel",)), name="to_bf16",
    )(w)


def _rot_half_cols(w):
    half = w.shape[-1] // 2
    return jnp.concatenate([-w[..., half:], w[..., :half]], axis=-1)


def _pad_cols(w, n):
    return jnp.pad(w, [(0, 0)] * (w.ndim - 1) + [(0, n - w.shape[-1])])


IN_WIDTHS = (Q_LORA_RANK, KV_LORA_RANK, 2 * CONV_CHANNELS, 3 * GDN_W, GDN_W, LANES, LANES)


def _layout_w_in(w_in):
    sizes = (Q_LORA_RANK, KV_LORA_RANK, QK_ROPE_DIM, 2 * CONV_CHANNELS, 3 * GDN_W, GDN_W, 4 * GDN_HEADS)
    c_q, c_kv, k_rope, conv, gqkv, gz, gates = jnp.split(w_in, np.cumsum(sizes)[:-1].tolist(), axis=1)
    misc = _pad_cols(jnp.concatenate([k_rope, gates], axis=1), LANES)
    krot = _pad_cols(_rot_half_cols(k_rope), LANES)
    return jnp.concatenate([c_q, c_kv, conv, gqkv, gz, misc, krot], axis=1).astype(BF16)


def _layout_w_uq(w_uq):
    r = w_uq.shape[0]
    w = w_uq.reshape(r, MLA_HEADS, QK_HEAD_DIM)
    nope = w[..., :QK_NOPE_DIM]
    rope = w[..., QK_NOPE_DIM:]
    parts = [nope, _pad_cols(rope, LANES), _pad_cols(_rot_half_cols(rope), LANES)]
    return jnp.concatenate([p.reshape(r, MLA_HEADS * LANES) for p in parts], axis=1).astype(BF16)


def _layout_w_ukv(w_ukv):
    r = w_ukv.shape[0]
    w = w_ukv.reshape(r, MLA_HEADS, QK_NOPE_DIM + V_HEAD_DIM)
    return jnp.concatenate([w[..., :QK_NOPE_DIM].reshape(r, -1), w[..., QK_NOPE_DIM:].reshape(r, -1)],
                           axis=1).astype(BF16)


def _layer(x, cos, sin, batch, seq, pre_mix_norm, w_in, q_a_norm, w_uq, kv_a_norm, w_ukv, conv_dw_w,
           conv_dw_b, conv_ln_g, conv_ln_b, gdn_conv_w, gdn_a_log, gdn_dt_bias, gdn_out_norm, w_out,
           post_mix_norm, pre_ffn_norm, w_gate, w_up, w_down, post_ffn_norm):
    cq, ckv, conv_in, gqkv, gz, misc, krot = _in_proj(x, pre_mix_norm, _layout_w_in(w_in), IN_WIDTHS)
    q_scale = QK_HEAD_DIM ** -0.5 * float(np.log2(np.e))
    q, k, v = _mla_prep(cq, ckv, misc, krot, cos, sin, q_a_norm, kv_a_norm,
                        _layout_w_uq(w_uq), _layout_w_ukv(w_ukv), q_scale)
    o_a = _attention(q, k, v, batch, seq)
    o_b = _conformer(conv_in, conv_dw_w, conv_dw_b, conv_ln_g, conv_ln_b, seq)
    gq, gk, gv, gates = _gdn_prep(gqkv, misc, gdn_conv_w, gdn_a_log, gdn_dt_bias, seq)
    o_f, o_bw = _gdn(gq, gk, gv, gates, batch, seq)
    x = _out_proj(x, o_a, o_b, o_f, o_bw, gz, gdn_out_norm, _to_bf16(w_out), post_mix_norm)
    return _ffn(x, pre_ffn_norm, _to_bf16(w_gate), _to_bf16(w_up), _to_bf16(w_down), post_ffn_norm)


def kernel(x, positions, pre_mix_norm, w_in, q_a_norm, w_uq, kv_a_norm, w_ukv, conv_dw_w, conv_dw_b,
           conv_ln_g, conv_ln_b, gdn_conv_w, gdn_a_log, gdn_dt_bias, gdn_out_norm, w_out, post_mix_norm,
           pre_ffn_norm, w_gate, w_up, w_down, post_ffn_norm):
    batch, seq, d = x.shape
    cos, sin = _rope_tables(positions)
    h = x.reshape(batch * seq, d)
    per_layer = (pre_mix_norm, w_in, q_a_norm, w_uq, kv_a_norm, w_ukv, conv_dw_w, conv_dw_b, conv_ln_g,
                 conv_ln_b, gdn_conv_w, gdn_a_log, gdn_dt_bias, gdn_out_norm, w_out, post_mix_norm,
                 pre_ffn_norm, w_gate, w_up, w_down, post_ffn_norm)
    for l in range(pre_mix_norm.shape[0]):
        h = _layer(h, cos, sin, batch, seq, *(p[l] for p in per_layer))
    return h.reshape(batch, seq, d)
```

```python
import functools

import numpy as np
import jax
import jax.numpy as jnp
from jax import lax
from jax.experimental import pallas as pl
from jax.experimental.pallas import tpu as pltpu

F32 = jnp.float32
BF16 = jnp.bfloat16

EPS = 1e-6
LANES = 128
SUBLANES = 8
VMEM_LIMIT = 56 * 1024 * 1024
FFN_VMEM_LIMIT = 62 * 1024 * 1024

MLA_HEADS = 8
Q_LORA_RANK = 512
KV_LORA_RANK = 512
QK_NOPE_DIM = 128
QK_ROPE_DIM = 64
QK_HEAD_DIM = QK_NOPE_DIM + QK_ROPE_DIM
V_HEAD_DIM = 128
ROPE_THETA = 10000.0
QK_PAD_DIM = 2 * LANES
V_PAD_DIM = 2 * LANES
CONV_CHANNELS = 512
CONV_WIDTH = 31
GDN_HEADS = 4
GDN_K_DIM = 128
GDN_V_DIM = 128
GDN_SHORT_CONV = 5
GDN_W = GDN_HEADS * GDN_K_DIM
GDN_CHUNK = 256
GATE_LANE0 = QK_ROPE_DIM


def _params(semantics, vmem_limit=VMEM_LIMIT):
    return pltpu.CompilerParams(dimension_semantics=semantics, vmem_limit_bytes=vmem_limit)


def _resident(shape):
    return pl.BlockSpec(shape, lambda *_: (0,) * len(shape), pipeline_mode=pl.Buffered(1))


def _rms(x, w):
    return x * lax.rsqrt(jnp.mean(x * x, axis=-1, keepdims=True) + EPS) * w


def _sigmoid(x):
    return 1.0 / (1.0 + jnp.exp(-x))


def _silu(x):
    return x * _sigmoid(x)


def _rope_table_kernel(pos_ref, inv_ref, cos_ref, sin_ref):
    ang = pos_ref[...].astype(F32) * inv_ref[...]
    lane = lax.broadcasted_iota(jnp.int32, ang.shape, 1)
    valid = lane < QK_ROPE_DIM
    cos_ref[...] = jnp.where(valid, jnp.cos(ang), 0.0)
    sin_ref[...] = jnp.where(valid, jnp.sin(ang), 0.0)


def _rope_tables(positions):
    t = positions.size
    tm = min(t, 2048)
    half = QK_ROPE_DIM // 2
    inv = 1.0 / (ROPE_THETA ** (jnp.arange(0, QK_ROPE_DIM, 2, dtype=F32) / QK_ROPE_DIM))
    inv_row = jnp.zeros((1, LANES), F32).at[0, :half].set(inv).at[0, half:2 * half].set(inv)
    spec = pl.BlockSpec((tm, LANES), lambda i: (i, 0))
    return pl.pallas_call(
        _rope_table_kernel,
        grid=(t // tm,),
        in_specs=[pl.BlockSpec((tm, 1), lambda i: (i, 0)), _resident((1, LANES))],
        out_specs=[spec, spec],
        out_shape=[jax.ShapeDtypeStruct((t, LANES), F32)] * 2,
        compiler_params=_params(("parallel",)),
        name="rope_tables",
    )(positions.reshape(t, 1), inv_row)


def _in_proj_kernel(x_ref, g_ref, w_ref, *out_refs):
    half = x_ref.shape[0] // 2
    for r in (0, half):
        rows = slice(r, r + half)
        xn = _rms(x_ref[rows, :], g_ref[...]).astype(BF16)
        off = 0
        for o_ref in out_refs:
            n = o_ref.shape[1]
            o_ref[rows, :] = jnp.dot(xn, w_ref[:, off:off + n],
                                     preferred_element_type=F32).astype(o_ref.dtype)
            off += n


def _in_proj(x, gain, w, widths, tm=512):
    t, d = x.shape
    n = w.shape[1]
    assert sum(widths) == n
    return pl.pallas_call(
        _in_proj_kernel,
        grid=(t // tm,),
        in_specs=[pl.BlockSpec((tm, d), lambda i: (i, 0)), _resident((1, d)), _resident((d, n))],
        out_specs=[pl.BlockSpec((tm, wd), lambda i: (i, 0)) for wd in widths],
        out_shape=[jax.ShapeDtypeStruct((t, wd), F32) for wd in widths],
        compiler_params=_params(("parallel",)),
        name="in_proj",
    )(x, gain.reshape(1, d), w)


def _mla_prep_kernel(cq_ref, ckv_ref, misc_ref, krot_ref, cos_ref, sin_ref, qn_ref, kvn_ref,
                     wq_ref, wkv_ref, q_out, k_out, v_out, *, q_scale):
    cos = cos_ref[...]
    sin = sin_ref[...]
    hw = MLA_HEADS * LANES
    cqn = _rms(cq_ref[...], qn_ref[...]).astype(BF16)
    q_nope = jnp.dot(cqn, wq_ref[:, :hw], preferred_element_type=F32)
    q_rope = jnp.dot(cqn, wq_ref[:, hw:2 * hw], preferred_element_type=F32)
    q_rot = jnp.dot(cqn, wq_ref[:, 2 * hw:], preferred_element_type=F32)
    ckvn = _rms(ckv_ref[...], kvn_ref[...]).astype(BF16)
    k_nope = jnp.dot(ckvn, wkv_ref[:, :hw], preferred_element_type=F32)
    v = jnp.dot(ckvn, wkv_ref[:, hw:], preferred_element_type=F32)
    k_rope = (misc_ref[...] * cos + krot_ref[...] * sin).astype(k_out.dtype)
    ones_col = jnp.where(lax.broadcasted_iota(jnp.int32, cos.shape, 1) == 0, 1.0, 0.0).astype(v_out.dtype)
    for h in range(MLA_HEADS):
        v_out[:, h * V_PAD_DIM:h * V_PAD_DIM + V_HEAD_DIM] = (
            v[:, h * V_HEAD_DIM:(h + 1) * V_HEAD_DIM].astype(v_out.dtype))
        v_out[:, h * V_PAD_DIM + V_HEAD_DIM:(h + 1) * V_PAD_DIM] = ones_col
        blk = slice(h * LANES, (h + 1) * LANES)
        lo = slice(h * QK_PAD_DIM, h * QK_PAD_DIM + LANES)
        hi = slice(h * QK_PAD_DIM + LANES, (h + 1) * QK_PAD_DIM)
        q_out[:, lo] = (q_nope[:, blk] * q_scale).astype(q_out.dtype)
        q_out[:, hi] = ((q_rope[:, blk] * cos + q_rot[:, blk] * sin) * q_scale).astype(q_out.dtype)
        k_out[:, lo] = k_nope[:, blk].astype(k_out.dtype)
        k_out[:, hi] = k_rope


def _mla_prep(cq, ckv, misc, krot, cos, sin, q_a_norm, kv_a_norm, wq, wkv, q_scale, tm=512):
    t = cq.shape[0]
    row = lambda n: pl.BlockSpec((tm, n), lambda i: (i, 0))
    qk_w = MLA_HEADS * QK_PAD_DIM
    v_w = MLA_HEADS * V_PAD_DIM
    return pl.pallas_call(
        functools.partial(_mla_prep_kernel, q_scale=q_scale),
        grid=(t // tm,),
        in_specs=[row(Q_LORA_RANK), row(KV_LORA_RANK), row(LANES), row(LANES), row(LANES), row(LANES),
                  _resident((1, Q_LORA_RANK)), _resident((1, KV_LORA_RANK)),
                  _resident(wq.shape), _resident(wkv.shape)],
        out_specs=[row(qk_w), row(qk_w), row(v_w)],
        out_shape=[jax.ShapeDtypeStruct((t, qk_w), BF16), jax.ShapeDtypeStruct((t, qk_w), BF16),
                   jax.ShapeDtypeStruct((t, v_w), BF16)],
        compiler_params=_params(("parallel",)),
        name="mla_prep",
    )(cq, ckv, misc, krot, cos, sin, q_a_norm.reshape(1, -1), kv_a_norm.reshape(1, -1), wq, wkv)


ATTN_TK = 512


def _attn_kernel(q_ref, k_ref, v_ref, o_ref, *, heads):
    tq = q_ref.shape[0]
    m = [jnp.full((tq, 1), -jnp.inf, F32)] * heads
    acc = [jnp.zeros((tq, V_PAD_DIM), F32)] * heads
    for j in range(k_ref.shape[0] // ATTN_TK):
        rows = slice(j * ATTN_TK, (j + 1) * ATTN_TK)
        for h in range(heads):
            qk_cols = slice(h * QK_PAD_DIM, (h + 1) * QK_PAD_DIM)
            v_cols = slice(h * V_PAD_DIM, (h + 1) * V_PAD_DIM)
            s = lax.dot_general(q_ref[:, qk_cols], k_ref[rows, qk_cols], (((1,), (1,)), ((), ())),
                                preferred_element_type=F32)
            m_new = jnp.maximum(m[h], jnp.max(s, axis=-1, keepdims=True))
            p = jnp.exp2(s - m_new).astype(BF16)
            acc[h] = acc[h] * jnp.exp2(m[h] - m_new) + jnp.dot(p, v_ref[rows, v_cols],
                                                               preferred_element_type=F32)
            m[h] = m_new
    for h in range(heads):
        o_ref[:, h * V_HEAD_DIM:(h + 1) * V_HEAD_DIM] = (
            acc[h][:, :V_HEAD_DIM] / acc[h][:, V_HEAD_DIM:V_HEAD_DIM + 1]).astype(o_ref.dtype)


def _attention(q, k, v, batch, seq, tq=1024, heads=4):
    t = q.shape[0]
    nq = seq // tq
    return pl.pallas_call(
        functools.partial(_attn_kernel, heads=heads),
        grid=(batch, MLA_HEADS // heads, nq),
        in_specs=[pl.BlockSpec((tq, heads * QK_PAD_DIM), lambda b, h, i: (b * nq + i, h)),
                  pl.BlockSpec((seq, heads * QK_PAD_DIM), lambda b, h, i: (b, h)),
                  pl.BlockSpec((seq, heads * V_PAD_DIM), lambda b, h, i: (b, h))],
        out_specs=pl.BlockSpec((tq, heads * V_HEAD_DIM), lambda b, h, i: (b * nq + i, h)),
        out_shape=jax.ShapeDtypeStruct((t, MLA_HEADS * V_HEAD_DIM), BF16),
        compiler_params=_params(("parallel", "parallel", "arbitrary")),
        name="mla_attention",
    )(q, k, v)


def _halo_specs(tm, halo, width, n_rows):
    per = tm // halo
    last = n_rows // halo - 1
    return [pl.BlockSpec((tm, width), lambda i: (i, 0)),
            pl.BlockSpec((halo, width), lambda i: (jnp.maximum(i * per - 1, 0), 0)),
            pl.BlockSpec((halo, width), lambda i: (jnp.minimum((i + 1) * per, last), 0))]


def _seq_edges(tiles_per_seq):
    i = pl.program_id(0) % tiles_per_seq
    return i == 0, i == tiles_per_seq - 1


CONF_HALO = 16
CONF_ROWS = 32


def _conformer_kernel(cur_ref, prev_ref, next_ref, w_ref, b_ref, g_ref, beta_ref, o_ref, ext_ref,
                      shift_ref, *, tiles_per_seq):
    tm = cur_ref.shape[0]
    c = CONV_CHANNELS
    first, last = _seq_edges(tiles_per_seq)

    def glu(u):
        return u[:, :c] * _sigmoid(u[:, c:])

    ext_ref[0:CONF_HALO, :] = jnp.where(first, 0.0, glu(prev_ref[...]))
    ext_ref[CONF_HALO:CONF_HALO + tm, :] = glu(cur_ref[...])
    ext_ref[CONF_HALO + tm:, :] = jnp.where(last, 0.0, glu(next_ref[...]))
    n_shift = shift_ref.shape[1]
    for s in range(1, SUBLANES):
        shift_ref[s, :, :] = ext_ref[s:s + n_shift, :]
    pad = CONV_WIDTH // 2
    bias = b_ref[...]
    for r in range(0, tm, CONF_ROWS):
        acc = jnp.broadcast_to(bias, (CONF_ROWS, c))
        for d in range(CONV_WIDTH):
            start = CONF_HALO + r - pad + d
            s = start % SUBLANES
            base = start - s
            win = ext_ref[base:base + CONF_ROWS, :] if s == 0 else shift_ref[s, base:base + CONF_ROWS, :]
            acc = acc + win * w_ref[d:d + 1, :]
        mu = jnp.mean(acc, axis=-1, keepdims=True)
        cen = acc - mu
        var = jnp.mean(cen * cen, axis=-1, keepdims=True)
        y = cen * lax.rsqrt(var + EPS) * g_ref[...] + beta_ref[...]
        o_ref[r:r + CONF_ROWS, :] = _silu(y).astype(o_ref.dtype)


def _conformer(conv_in, dw_w, dw_b, ln_g, ln_b, seq, tm=512):
    t = conv_in.shape[0]
    c = CONV_CHANNELS
    return pl.pallas_call(
        functools.partial(_conformer_kernel, tiles_per_seq=seq // tm),
        grid=(t // tm,),
        in_specs=_halo_specs(tm, CONF_HALO, 2 * c, t) + [
            _resident((CONV_WIDTH, c)), _resident((1, c)), _resident((1, c)), _resident((1, c))],
        out_specs=pl.BlockSpec((tm, c), lambda i: (i, 0)),
        out_shape=jax.ShapeDtypeStruct((t, c), BF16),
        scratch_shapes=[pltpu.VMEM((tm + 2 * CONF_HALO, c), F32),
                        pltpu.VMEM((SUBLANES, tm + 2 * CONF_HALO - SUBLANES, c), F32)],
        compiler_params=_params(("parallel",)),
        name="conformer_conv",
    )(conv_in, conv_in, conv_in, dw_w.reshape(CONV_WIDTH, c), dw_b.reshape(1, c),
      ln_g.reshape(1, c), ln_b.reshape(1, c))


GDN_HALO = 8
GDN_ROWS = 64


def _gdn_prep_kernel(cur_ref, prev_ref, next_ref, misc_ref, w_ref, alog_ref, dtb_ref,
                     q_out, k_out, v_out, gate_out, ext_ref, *, tiles_per_seq):
    tm = cur_ref.shape[0]
    first, last = _seq_edges(tiles_per_seq)
    ext_ref[0:GDN_HALO, :] = jnp.where(first, 0.0, prev_ref[...])
    ext_ref[GDN_HALO:GDN_HALO + tm, :] = cur_ref[...]
    ext_ref[GDN_HALO + tm:, :] = jnp.where(last, 0.0, next_ref[...])
    pad = GDN_SHORT_CONV // 2
    for r in range(0, tm, GDN_ROWS):
        rows = slice(r, r + GDN_ROWS)
        for part, out in enumerate((q_out, k_out, v_out)):
            cols = slice(part * GDN_W, (part + 1) * GDN_W)
            acc = jnp.zeros((GDN_ROWS, GDN_W), F32)
            for d in range(GDN_SHORT_CONV):
                start = GDN_HALO + r - pad + d
                acc = acc + ext_ref[start:start + GDN_ROWS, cols] * w_ref[d:d + 1, cols]
            y = _silu(acc)
            if part == 2:
                out[rows, :] = y
                continue
            scale = GDN_K_DIM ** -0.5 if part == 0 else 1.0
            for h in range(GDN_HEADS):
                blk = slice(h * GDN_K_DIM, (h + 1) * GDN_K_DIM)
                yh = y[:, blk]
                inv = lax.rsqrt(jnp.sum(yh * yh, axis=-1, keepdims=True) + EPS)
                out[rows, blk] = yh * (inv * scale)
    m = misc_ref[...]
    lane = lax.broadcasted_iota(jnp.int32, m.shape, 1) - GATE_LANE0
    is_gate = (lane >= 0) & (lane < 4 * GDN_HEADS)
    is_a = is_gate & ((lane // GDN_HEADS) % 2 == 0)
    z = m + dtb_ref[...]
    softplus = jnp.maximum(z, 0.0) + jnp.log(1.0 + jnp.exp(-jnp.abs(z)))
    g = -jnp.exp(alog_ref[...]) * softplus
    gate_out[...] = jnp.where(is_a, g, jnp.where(is_gate, _sigmoid(m), 0.0))


def _gdn_prep(gqkv, misc, conv_w, a_log, dt_bias, seq, tm=512):
    t = gqkv.shape[0]
    w3 = 3 * GDN_W
    zeros = jnp.zeros((GDN_HEADS,), F32)
    lane_vals = lambda p: jnp.zeros((1, LANES), F32).at[0, GATE_LANE0:GATE_LANE0 + 4 * GDN_HEADS].set(
        jnp.concatenate([p[0], zeros, p[1], zeros]))
    row = lambda n: pl.BlockSpec((tm, n), lambda i: (i, 0))
    return pl.pallas_call(
        functools.partial(_gdn_prep_kernel, tiles_per_seq=seq // tm),
        grid=(t // tm,),
        in_specs=_halo_specs(tm, GDN_HALO, w3, t) + [
            row(LANES), _resident((GDN_SHORT_CONV, w3)), _resident((1, LANES)), _resident((1, LANES))],
        out_specs=[row(GDN_W), row(GDN_W), row(GDN_W), row(LANES)],
        out_shape=[jax.ShapeDtypeStruct((t, GDN_W), F32)] * 3 + [jax.ShapeDtypeStruct((t, LANES), F32)],
        scratch_shapes=[pltpu.VMEM((tm + 2 * GDN_HALO, w3), F32)],
        compiler_params=_params(("parallel",)),
        name="gdn_prep",
    )(gqkv, gqkv, gqkv, misc, conv_w.reshape(GDN_SHORT_CONV, w3), lane_vals(a_log), lane_vals(dt_bias))


def _bdot(a, b):
    return jnp.dot(a.astype(BF16), b.astype(BF16), preferred_element_type=F32)


def _bdot_nt(a, b):
    return lax.dot_general(a.astype(BF16), b.astype(BF16), (((1,), (1,)), ((), ())),
                           preferred_element_type=F32)


def _gdn_kernel(qf_ref, kf_ref, vf_ref, gf_ref, qb_ref, kb_ref, vb_ref, gb_ref, of_ref, ob_ref,
                state_ref, *, heads):
    c = qf_ref.shape[0]
    d = GDN_K_DIM

    @pl.when(pl.program_id(2) == 0)
    def _():
        state_ref[...] = jnp.zeros_like(state_ref)

    ri = lax.broadcasted_iota(jnp.int32, (c, c), 0)
    ci = lax.broadcasted_iota(jnp.int32, (c, c), 1)
    xor = ri ^ ci
    lane_row = lax.broadcasted_iota(jnp.int32, (1, LANES), 1)
    lane_col = lax.broadcasted_iota(jnp.int32, (LANES, 1), 0)

    chains = []
    for direction, (q_ref, k_ref, v_ref, g_ref, o_ref) in enumerate(
            ((qf_ref, kf_ref, vf_ref, gf_ref, of_ref), (qb_ref, kb_ref, vb_ref, gb_ref, ob_ref))):
        lower = direction == 0
        incl = (ri >= ci) if lower else (ri <= ci)
        strict = (ri > ci) if lower else (ri < ci)
        gates = g_ref[...]
        gc_all = jnp.dot(jnp.where(incl, 1.0, 0.0), gates, preferred_element_type=F32,
                         precision=lax.Precision.HIGHEST)
        gc_all_t = gc_all.T
        last = c - 1 if lower else 0
        for h in range(heads):
            blk = slice(h * d, (h + 1) * d)
            g_lane = GATE_LANE0 + 2 * GDN_HEADS * direction + pl.program_id(1) * heads + h
            gc = jnp.sum(jnp.where(lane_row == g_lane, gc_all, 0.0), axis=1, keepdims=True)
            beta = jnp.sum(jnp.where(lane_row == g_lane + GDN_HEADS, gates, 0.0), axis=1, keepdims=True)
            gc_row = jnp.sum(jnp.where(lane_col == g_lane, gc_all_t, 0.0), axis=0, keepdims=True)
            chains.append(dict(q=q_ref[:, blk], k=k_ref[:, blk], v=v_ref[:, blk], gc=gc, beta=beta,
                               gc_row=gc_row, g_tot=gc[last:last + 1, :], incl=incl, strict=strict,
                               o_ref=o_ref, blk=blk, idx=(direction, h)))

    for ch in chains:
        ch["kk"] = _bdot_nt(ch["k"], ch["k"])
    for ch in chains:
        ch["qk"] = _bdot_nt(ch["q"], ch["k"])
    for ch in chains:
        incl = ch["incl"]
        decay = jnp.where(incl, jnp.exp(jnp.where(incl, ch["gc"] - ch["gc_row"], 0.0)), 0.0)
        ch["a"] = jnp.where(ch["strict"], ch["kk"] * ch["beta"] * decay, 0.0)
        ch["qkd"] = (ch["qk"] * decay).astype(BF16)
        ch["t"] = jnp.where(ri == ci, 1.0, 0.0) - jnp.where(xor < 2, ch["a"], 0.0)
    b = 2
    while b < c:
        level = (xor >= b) & (xor < 2 * b)
        for ch in chains:
            ch["x"] = _bdot(jnp.where(level, ch["a"], 0.0), ch["t"])
        for ch in chains:
            ch["t"] = ch["t"] - _bdot(ch["t"], ch["x"])
        b *= 2
    for ch in chains:
        rhs = jnp.concatenate([ch["v"] * ch["beta"], ch["k"] * (ch["beta"] * jnp.exp(ch["gc"]))], axis=1)
        ch["uw"] = _bdot(ch["t"], rhs).astype(BF16)
    for ch in chains:
        ch["z"] = _bdot(ch["qkd"], ch["uw"])
    for ch in chains:
        k_til = ch["k"] * jnp.exp(ch["g_tot"] - ch["gc"])
        ch["y"] = _bdot(k_til.T, ch["uw"])
    for ch in chains:
        state = state_ref[ch["idx"]]
        q_eff = ch["q"] * jnp.exp(ch["gc"]) - ch["z"][:, d:]
        ch["o_ref"][:, ch["blk"]] = _bdot(q_eff, state) + ch["z"][:, :d]
        state_ref[ch["idx"]] = (state * jnp.exp(ch["g_tot"]) - _bdot(ch["y"][:, d:], state)
                                + ch["y"][:, :d])


def _gdn(q, k, v, gates, batch, seq, heads=GDN_HEADS):
    t = q.shape[0]
    c = GDN_CHUNK
    nc = seq // c
    hw = heads * GDN_K_DIM
    gate_f = pl.BlockSpec((c, LANES), lambda b, hp, i: (b * nc + i, 0))
    gate_b = pl.BlockSpec((c, LANES), lambda b, hp, i: (b * nc + nc - 1 - i, 0))
    head_f = pl.BlockSpec((c, hw), lambda b, hp, i: (b * nc + i, hp))
    head_b = pl.BlockSpec((c, hw), lambda b, hp, i: (b * nc + nc - 1 - i, hp))
    return pl.pallas_call(
        functools.partial(_gdn_kernel, heads=heads),
        grid=(batch, GDN_HEADS // heads, nc),
        in_specs=[head_f, head_f, head_f, gate_f, head_b, head_b, head_b, gate_b],
        out_specs=[head_f, head_b],
        out_shape=[jax.ShapeDtypeStruct((t, GDN_W), F32)] * 2,
        scratch_shapes=[pltpu.VMEM((2, heads, GDN_K_DIM, GDN_V_DIM), F32)],
        compiler_params=_params(("parallel", "parallel", "arbitrary")),
        name="gdn_delta_rule",
    )(q, k, v, gates, q, k, v, gates)


def _out_proj_kernel(x_ref, oa_ref, ob_ref, of_ref, obw_ref, z_ref, onorm_ref, w_ref, g_ref, o_ref):
    na = oa_ref.shape[1]
    nb = ob_ref.shape[1]
    half = x_ref.shape[0] // 2
    for r in (0, half):
        rows = slice(r, r + half)
        oc = of_ref[rows, :] + obw_ref[rows, :]
        z = z_ref[rows, :]
        parts = []
        for h in range(GDN_HEADS):
            blk = slice(h * GDN_V_DIM, (h + 1) * GDN_V_DIM)
            parts.append(_rms(oc[:, blk], onorm_ref[...]) * _silu(z[:, blk]))
        og = jnp.concatenate(parts, axis=1).astype(BF16)
        mix = (jnp.dot(oa_ref[rows, :], w_ref[:na, :], preferred_element_type=F32)
               + jnp.dot(ob_ref[rows, :], w_ref[na:na + nb, :], preferred_element_type=F32)
               + jnp.dot(og, w_ref[na + nb:, :], preferred_element_type=F32))
        o_ref[rows, :] = x_ref[rows, :] + _rms(mix, g_ref[...])


def _out_proj(x, o_a, o_b, o_f, o_bw, z, out_norm, w_out, gain, tm=512):
    t, d = x.shape
    row = lambda n: pl.BlockSpec((tm, n), lambda i: (i, 0))
    return pl.pallas_call(
        _out_proj_kernel,
        grid=(t // tm,),
        in_specs=[row(d), row(o_a.shape[1]), row(o_b.shape[1]), row(GDN_W), row(GDN_W), row(GDN_W),
                  _resident((1, GDN_V_DIM)), _resident(w_out.shape), _resident((1, d))],
        out_specs=row(d),
        out_shape=jax.ShapeDtypeStruct((t, d), F32),
        compiler_params=_params(("parallel",)),
        name="out_proj",
    )(x, o_a, o_b, o_f, o_bw, z, out_norm.reshape(1, -1), w_out, gain.reshape(1, d))


def _ffn_kernel(x_ref, gin_ref, wg_ref, wu_ref, wd_ref, gout_ref, o_ref, hn_ref):
    j = pl.program_id(1)

    @pl.when(j == 0)
    def _():
        hn_ref[...] = _rms(x_ref[...], gin_ref[...]).astype(BF16)
        o_ref[...] = jnp.zeros_like(o_ref)

    hn = hn_ref[...]
    gate = jnp.dot(hn, wg_ref[...], preferred_element_type=F32)
    up = jnp.dot(hn, wu_ref[...], preferred_element_type=F32)
    o_ref[...] += jnp.dot((_silu(gate) * up).astype(BF16), wd_ref[...], preferred_element_type=F32)

    @pl.when(j == pl.num_programs(1) - 1)
    def _():
        o_ref[...] = x_ref[...] + _rms(o_ref[...], gout_ref[...])


def _ffn(x, gain_in, w_gate, w_up, w_down, gain_out, tm=1024, tf=512):
    t, d = x.shape
    f = w_gate.shape[1]
    tm = min(tm, t)
    return pl.pallas_call(
        _ffn_kernel,
        grid=(t // tm, f // tf),
        in_specs=[pl.BlockSpec((tm, d), lambda i, j: (i, 0)),
                  _resident((1, d)),
                  pl.BlockSpec((d, tf), lambda i, j: (0, j)),
                  pl.BlockSpec((d, tf), lambda i, j: (0, j)),
                  pl.BlockSpec((tf, d), lambda i, j: (j, 0)),
                  _resident((1, d))],
        out_specs=pl.BlockSpec((tm, d), lambda i, j: (i, 0)),
        out_shape=jax.ShapeDtypeStruct((t, d), F32),
        scratch_shapes=[pltpu.VMEM((tm, d), BF16)],
        compiler_params=_params(("parallel", "arbitrary"), FFN_VMEM_LIMIT),
        name="swiglu_ffn",
    )(x, gain_in.reshape(1, d), w_gate, w_up, w_down, gain_out.reshape(1, d))


def _cast_kernel(x_ref, o_ref):
    o_ref[...] = x_ref[...].astype(o_ref.dtype)


def _to_bf16(w_stack, layer, rows=256):
    _, r, c = w_stack.shape
    return pl.pallas_call(
        _cast_kernel, grid=(r // rows,),
        in_specs=[pl.BlockSpec((None, rows, c), lambda i: (layer, i, 0))],
        out_specs=pl.BlockSpec((rows, c), lambda i: (i, 0)),
        out_shape=jax.ShapeDtypeStruct((r, c), BF16),
        compiler_params=_params(("parallel",)), name="to_bf16",
    )(w_stack)


def _rot_half_cols(w):
    half = w.shape[-1] // 2
    return jnp.concatenate([-w[..., half:], w[..., :half]], axis=-1)


def _pad_cols(w, n):
    return jnp.pad(w, [(0, 0)] * (w.ndim - 1) + [(0, n - w.shape[-1])])


IN_WIDTHS = (Q_LORA_RANK, KV_LORA_RANK, 2 * CONV_CHANNELS, 3 * GDN_W, GDN_W, LANES, LANES)


def _layout_w_in(w_in):
    sizes = (Q_LORA_RANK, KV_LORA_RANK, QK_ROPE_DIM, 2 * CONV_CHANNELS, 3 * GDN_W, GDN_W, 4 * GDN_HEADS)
    c_q, c_kv, k_rope, conv, gqkv, gz, gates = jnp.split(w_in, np.cumsum(sizes)[:-1].tolist(), axis=1)
    misc = _pad_cols(jnp.concatenate([k_rope, gates], axis=1), LANES)
    krot = _pad_cols(_rot_half_cols(k_rope), LANES)
    return jnp.concatenate([c_q, c_kv, conv, gqkv, gz, misc, krot], axis=1).astype(BF16)


def _layout_w_uq(w_uq):
    r = w_uq.shape[0]
    w = w_uq.reshape(r, MLA_HEADS, QK_HEAD_DIM)
    nope = w[..., :QK_NOPE_DIM]
    rope = w[..., QK_NOPE_DIM:]
    parts = [nope, _pad_cols(rope, LANES), _pad_cols(_rot_half_cols(rope), LANES)]
    return jnp.concatenate([p.reshape(r, MLA_HEADS * LANES) for p in parts], axis=1).astype(BF16)


def _layout_w_ukv(w_ukv):
    r = w_ukv.shape[0]
    w = w_ukv.reshape(r, MLA_HEADS, QK_NOPE_DIM + V_HEAD_DIM)
    return jnp.concatenate([w[..., :QK_NOPE_DIM].reshape(r, -1), w[..., QK_NOPE_DIM:].reshape(r, -1)],
                           axis=1).astype(BF16)


def _layer(x, cos, sin, batch, seq, pre_mix_norm, w_in, q_a_norm, w_uq, kv_a_norm, w_ukv, conv_dw_w,
           conv_dw_b, conv_ln_g, conv_ln_b, gdn_conv_w, gdn_a_log, gdn_dt_bias, gdn_out_norm, w_out,
           post_mix_norm, pre_ffn_norm, w_gate, w_up, w_down, post_ffn_norm):
    cq, ckv, conv_in, gqkv, gz, misc, krot = _in_proj(x, pre_mix_norm, _layout_w_in(w_in), IN_WIDTHS)
    q_scale = QK_HEAD_DIM ** -0.5 * float(np.log2(np.e))
    q, k, v = _mla_prep(cq, ckv, misc, krot, cos, sin, q_a_norm, kv_a_norm,
                        _layout_w_uq(w_uq), _layout_w_ukv(w_ukv), q_scale)
    o_a = _attention(q, k, v, batch, seq)
    o_b = _conformer(conv_in, conv_dw_w, conv_dw_b, conv_ln_g, conv_ln_b, seq)
    gq, gk, gv, gates = _gdn_prep(gqkv, misc, gdn_conv_w, gdn_a_log, gdn_dt_bias, seq)
    o_f, o_bw = _gdn(gq, gk, gv, gates, batch, seq)
    x = _out_proj(x, o_a, o_b, o_f, o_bw, gz, gdn_out_norm, w_out, post_mix_norm)
    return _ffn(x, pre_ffn_norm, w_gate, w_up, w_down, post_ffn_norm)


def kernel(x, positions, pre_mix_norm, w_in, q_a_norm, w_uq, kv_a_norm, w_ukv, conv_dw_w, conv_dw_b,
           conv_ln_g, conv_ln_b, gdn_conv_w, gdn_a_log, gdn_dt_bias, gdn_out_norm, w_out, post_mix_norm,
           pre_ffn_norm, w_gate, w_up, w_down, post_ffn_norm):
    batch, seq, d = x.shape
    cos, sin = _rope_tables(positions)
    h = x.reshape(batch * seq, d)
    per_layer = (pre_mix_norm, w_in, q_a_norm, w_uq, kv_a_norm, w_ukv, conv_dw_w, conv_dw_b, conv_ln_g,
                 conv_ln_b, gdn_conv_w, gdn_a_log, gdn_dt_bias, gdn_out_norm, w_out, post_mix_norm,
                 pre_ffn_norm, w_gate, w_up, w_down, post_ffn_norm)
    big = (w_out, w_gate, w_up, w_down)
    for l in range(pre_mix_norm.shape[0]):
        h = _layer(h, cos, sin, batch, seq,
                   *(_to_bf16(p, l) if any(p is b for b in big) else p[l] for p in per_layer))
    return h.reshape(batch, seq, d)
```

```python
import functools

import numpy as np
import jax
import jax.numpy as jnp
from jax import lax
from jax.experimental import pallas as pl
from jax.experimental.pallas import tpu as pltpu

F32 = jnp.float32
BF16 = jnp.bfloat16

EPS = 1e-6
LANES = 128
SUBLANES = 8
VMEM_LIMIT = 56 * 1024 * 1024
FFN_VMEM_LIMIT = 62 * 1024 * 1024

MLA_HEADS = 8
Q_LORA_RANK = 512
KV_LORA_RANK = 512
QK_NOPE_DIM = 128
QK_ROPE_DIM = 64
QK_HEAD_DIM = QK_NOPE_DIM + QK_ROPE_DIM
V_HEAD_DIM = 128
ROPE_THETA = 10000.0
QK_PAD_DIM = 2 * LANES
V_PAD_DIM = 2 * LANES
CONV_CHANNELS = 512
CONV_WIDTH = 31
GDN_HEADS = 4
GDN_K_DIM = 128
GDN_V_DIM = 128
GDN_SHORT_CONV = 5
GDN_W = GDN_HEADS * GDN_K_DIM
GDN_CHUNK = 256
GATE_LANE0 = QK_ROPE_DIM

def _params(semantics, vmem_limit=VMEM_LIMIT):
    return pltpu.CompilerParams(dimension_semantics=semantics, vmem_limit_bytes=vmem_limit)


def _resident(shape):
    return pl.BlockSpec(shape, lambda *_: (0,) * len(shape), pipeline_mode=pl.Buffered(1))


def _rms(x, w):
    return x * lax.rsqrt(jnp.mean(x * x, axis=-1, keepdims=True) + EPS) * w


def _sigmoid(x):
    return 1.0 / (1.0 + jnp.exp(-x))


def _silu(x):
    return x * _sigmoid(x)


def _rope_table_kernel(pos_ref, inv_ref, cos_ref, sin_ref):
    ang = pos_ref[...].astype(F32) * inv_ref[...]
    lane = lax.broadcasted_iota(jnp.int32, ang.shape, 1)
    valid = lane < QK_ROPE_DIM
    cos_ref[...] = jnp.where(valid, jnp.cos(ang), 0.0)
    sin_ref[...] = jnp.where(valid, jnp.sin(ang), 0.0)


def _rope_tables(positions):
    t = positions.size
    tm = min(t, 2048)
    half = QK_ROPE_DIM // 2
    inv = 1.0 / (ROPE_THETA ** (jnp.arange(0, QK_ROPE_DIM, 2, dtype=F32) / QK_ROPE_DIM))
    inv_row = jnp.zeros((1, LANES), F32).at[0, :half].set(inv).at[0, half:2 * half].set(inv)
    spec = pl.BlockSpec((tm, LANES), lambda i: (i, 0))
    return pl.pallas_call(
        _rope_table_kernel,
        grid=(t // tm,),
        in_specs=[pl.BlockSpec((tm, 1), lambda i: (i, 0)), _resident((1, LANES))],
        out_specs=[spec, spec],
        out_shape=[jax.ShapeDtypeStruct((t, LANES), F32)] * 2,
        compiler_params=_params(("parallel",)),
        name="rope_tables",
    )(positions.reshape(t, 1), inv_row)


def _in_proj_kernel(x_ref, g_ref, w_ref, *out_refs):
    half = x_ref.shape[0] // 2
    for r in (0, half):
        rows = slice(r, r + half)
        xn = _rms(x_ref[rows, :], g_ref[...]).astype(BF16)
        off = 0
        for o_ref in out_refs:
            n = o_ref.shape[1]
            o_ref[rows, :] = jnp.dot(xn, w_ref[:, off:off + n],
                                     preferred_element_type=F32).astype(o_ref.dtype)
            off += n


def _in_proj(x, gain, w, widths, tm=512):
    t, d = x.shape
    n = w.shape[1]
    assert sum(widths) == n
    return pl.pallas_call(
        _in_proj_kernel,
        grid=(t // tm,),
        in_specs=[pl.BlockSpec((tm, d), lambda i: (i, 0)), _resident((1, d)), _resident((d, n))],
        out_specs=[pl.BlockSpec((tm, wd), lambda i: (i, 0)) for wd in widths],
        out_shape=[jax.ShapeDtypeStruct((t, wd), F32) for wd in widths],
        compiler_params=_params(("parallel",)),
        name="in_proj",
    )(x, gain.reshape(1, d), w)


def _mla_prep_kernel(cq_ref, ckv_ref, misc_ref, krot_ref, cos_ref, sin_ref, qn_ref, kvn_ref,
                     wq_ref, wkv_ref, q_out, kn_out, kr_out, v_out, *, q_scale):
    cos = cos_ref[...]
    sin = sin_ref[...]
    hw = MLA_HEADS * LANES
    cqn = _rms(cq_ref[...], qn_ref[...]).astype(BF16)
    q_nope = jnp.dot(cqn, wq_ref[:, :hw], preferred_element_type=F32)
    q_rope = jnp.dot(cqn, wq_ref[:, hw:2 * hw], preferred_element_type=F32)
    q_rot = jnp.dot(cqn, wq_ref[:, 2 * hw:], preferred_element_type=F32)
    ckvn = _rms(ckv_ref[...], kvn_ref[...]).astype(BF16)
    kn_out[...] = jnp.dot(ckvn, wkv_ref[:, :hw], preferred_element_type=F32).astype(kn_out.dtype)
    v_out[...] = jnp.dot(ckvn, wkv_ref[:, hw:], preferred_element_type=F32).astype(v_out.dtype)
    kr_out[...] = (misc_ref[...] * cos + krot_ref[...] * sin).astype(kr_out.dtype)
    for h in range(MLA_HEADS):
        blk = slice(h * LANES, (h + 1) * LANES)
        lo = slice(h * QK_PAD_DIM, h * QK_PAD_DIM + LANES)
        hi = slice(h * QK_PAD_DIM + LANES, (h + 1) * QK_PAD_DIM)
        q_out[:, lo] = (q_nope[:, blk] * q_scale).astype(q_out.dtype)
        q_out[:, hi] = ((q_rope[:, blk] * cos + q_rot[:, blk] * sin) * q_scale).astype(q_out.dtype)


def _mla_prep(cq, ckv, misc, krot, cos, sin, q_a_norm, kv_a_norm, wq, wkv, q_scale, tm=512):
    t = cq.shape[0]
    row = lambda n: pl.BlockSpec((tm, n), lambda i: (i, 0))
    q_w = MLA_HEADS * QK_PAD_DIM
    kn_w = MLA_HEADS * QK_NOPE_DIM
    v_w = MLA_HEADS * V_HEAD_DIM
    bf = lambda n: jax.ShapeDtypeStruct((t, n), BF16)
    return pl.pallas_call(
        functools.partial(_mla_prep_kernel, q_scale=q_scale),
        grid=(t // tm,),
        in_specs=[row(Q_LORA_RANK), row(KV_LORA_RANK), row(LANES), row(LANES), row(LANES), row(LANES),
                  _resident((1, Q_LORA_RANK)), _resident((1, KV_LORA_RANK)),
                  _resident(wq.shape), _resident(wkv.shape)],
        out_specs=[row(q_w), row(kn_w), row(LANES), row(v_w)],
        out_shape=[bf(q_w), bf(kn_w), bf(LANES), bf(v_w)],
        compiler_params=_params(("parallel",)),
        name="mla_prep",
    )(cq, ckv, misc, krot, cos, sin, q_a_norm.reshape(1, -1), kv_a_norm.reshape(1, -1), wq, wkv)


ATTN_TK = 512


def _attn_kernel(q_ref, kn_ref, kr_ref, v_ref, o_ref, *, heads):
    tq = q_ref.shape[0]
    m = [jnp.full((tq, 1), -jnp.inf, F32)] * heads
    acc = [jnp.zeros((tq, V_PAD_DIM), F32)] * heads
    ones_col = jnp.where(lax.broadcasted_iota(jnp.int32, (ATTN_TK, LANES), 1) == 0, 1.0, 0.0).astype(BF16)
    for j in range(kn_ref.shape[0] // ATTN_TK):
        rows = slice(j * ATTN_TK, (j + 1) * ATTN_TK)
        k_rope = kr_ref[rows, :]
        for h in range(heads):
            head = slice(h * LANES, (h + 1) * LANES)
            k_op = jnp.concatenate([kn_ref[rows, head], k_rope], axis=1)
            v_op = jnp.concatenate([v_ref[rows, head], ones_col], axis=1)
            s = lax.dot_general(q_ref[:, h * QK_PAD_DIM:(h + 1) * QK_PAD_DIM], k_op,
                                (((1,), (1,)), ((), ())), preferred_element_type=F32)
            m_new = jnp.maximum(m[h], jnp.max(s, axis=-1, keepdims=True))
            p = jnp.exp2(s - m_new).astype(BF16)
            acc[h] = acc[h] * jnp.exp2(m[h] - m_new) + jnp.dot(p, v_op, preferred_element_type=F32)
            m[h] = m_new
    for h in range(heads):
        o_ref[:, h * V_HEAD_DIM:(h + 1) * V_HEAD_DIM] = (
            acc[h][:, :V_HEAD_DIM] / acc[h][:, V_HEAD_DIM:V_HEAD_DIM + 1]).astype(o_ref.dtype)


def _attention(q, k_nope, k_rope, v, batch, seq, tq=1024, heads=4):
    t = q.shape[0]
    nq = seq // tq
    return pl.pallas_call(
        functools.partial(_attn_kernel, heads=heads),
        grid=(batch, MLA_HEADS // heads, nq),
        in_specs=[pl.BlockSpec((tq, heads * QK_PAD_DIM), lambda b, h, i: (b * nq + i, h)),
                  pl.BlockSpec((seq, heads * QK_NOPE_DIM), lambda b, h, i: (b, h)),
                  pl.BlockSpec((seq, LANES), lambda b, h, i: (b, 0)),
                  pl.BlockSpec((seq, heads * V_HEAD_DIM), lambda b, h, i: (b, h))],
        out_specs=pl.BlockSpec((tq, heads * V_HEAD_DIM), lambda b, h, i: (b * nq + i, h)),
        out_shape=jax.ShapeDtypeStruct((t, MLA_HEADS * V_HEAD_DIM), BF16),
        compiler_params=_params(("parallel", "parallel", "arbitrary")),
        name="mla_attention",
    )(q, k_nope, k_rope, v)


def _halo_specs(tm, halo, width, n_rows):
    per = tm // halo
    last = n_rows // halo - 1
    return [pl.BlockSpec((tm, width), lambda i: (i, 0)),
            pl.BlockSpec((halo, width), lambda i: (jnp.maximum(i * per - 1, 0), 0)),
            pl.BlockSpec((halo, width), lambda i: (jnp.minimum((i + 1) * per, last), 0))]


def _seq_edges(tiles_per_seq):
    i = pl.program_id(0) % tiles_per_seq
    return i == 0, i == tiles_per_seq - 1


CONF_HALO = 16
CONF_ROWS = 32


def _conformer_kernel(cur_ref, prev_ref, next_ref, w_ref, b_ref, g_ref, beta_ref, o_ref, ext_ref,
                      shift_ref, *, tiles_per_seq):
    tm = cur_ref.shape[0]
    c = CONV_CHANNELS
    first, last = _seq_edges(tiles_per_seq)

    def glu(u):
        return u[:, :c] * _sigmoid(u[:, c:])

    ext_ref[0:CONF_HALO, :] = jnp.where(first, 0.0, glu(prev_ref[...]))
    ext_ref[CONF_HALO:CONF_HALO + tm, :] = glu(cur_ref[...])
    ext_ref[CONF_HALO + tm:, :] = jnp.where(last, 0.0, glu(next_ref[...]))
    n_shift = shift_ref.shape[1]
    for s in range(1, SUBLANES):
        shift_ref[s, :, :] = ext_ref[s:s + n_shift, :]
    pad = CONV_WIDTH // 2
    bias = b_ref[...]
    for r in range(0, tm, CONF_ROWS):
        acc = jnp.broadcast_to(bias, (CONF_ROWS, c))
        for d in range(CONV_WIDTH):
            start = CONF_HALO + r - pad + d
            s = start % SUBLANES
            base = start - s
            win = ext_ref[base:base + CONF_ROWS, :] if s == 0 else shift_ref[s, base:base + CONF_ROWS, :]
            acc = acc + win * w_ref[d:d + 1, :]
        mu = jnp.mean(acc, axis=-1, keepdims=True)
        cen = acc - mu
        var = jnp.mean(cen * cen, axis=-1, keepdims=True)
        y = cen * lax.rsqrt(var + EPS) * g_ref[...] + beta_ref[...]
        o_ref[r:r + CONF_ROWS, :] = _silu(y).astype(o_ref.dtype)


def _conformer(conv_in, dw_w, dw_b, ln_g, ln_b, seq, tm=512):
    t = conv_in.shape[0]
    c = CONV_CHANNELS
    return pl.pallas_call(
        functools.partial(_conformer_kernel, tiles_per_seq=seq // tm),
        grid=(t // tm,),
        in_specs=_halo_specs(tm, CONF_HALO, 2 * c, t) + [
            _resident((CONV_WIDTH, c)), _resident((1, c)), _resident((1, c)), _resident((1, c))],
        out_specs=pl.BlockSpec((tm, c), lambda i: (i, 0)),
        out_shape=jax.ShapeDtypeStruct((t, c), BF16),
        scratch_shapes=[pltpu.VMEM((tm + 2 * CONF_HALO, c), F32),
                        pltpu.VMEM((SUBLANES, tm + 2 * CONF_HALO - SUBLANES, c), F32)],
        compiler_params=_params(("parallel",)),
        name="conformer_conv",
    )(conv_in, conv_in, conv_in, dw_w.reshape(CONV_WIDTH, c), dw_b.reshape(1, c),
      ln_g.reshape(1, c), ln_b.reshape(1, c))


GDN_HALO = 8
GDN_ROWS = 64


def _gdn_prep_kernel(cur_ref, prev_ref, next_ref, misc_ref, w_ref, alog_ref, dtb_ref,
                     q_out, k_out, v_out, gate_out, ext_ref, *, tiles_per_seq):
    tm = cur_ref.shape[0]
    first, last = _seq_edges(tiles_per_seq)
    ext_ref[0:GDN_HALO, :] = jnp.where(first, 0.0, prev_ref[...])
    ext_ref[GDN_HALO:GDN_HALO + tm, :] = cur_ref[...]
    ext_ref[GDN_HALO + tm:, :] = jnp.where(last, 0.0, next_ref[...])
    pad = GDN_SHORT_CONV // 2
    for r in range(0, tm, GDN_ROWS):
        rows = slice(r, r + GDN_ROWS)
        for part, out in enumerate((q_out, k_out, v_out)):
            cols = slice(part * GDN_W, (part + 1) * GDN_W)
            acc = jnp.zeros((GDN_ROWS, GDN_W), F32)
            for d in range(GDN_SHORT_CONV):
                start = GDN_HALO + r - pad + d
                acc = acc + ext_ref[start:start + GDN_ROWS, cols] * w_ref[d:d + 1, cols]
            y = _silu(acc)
            if part == 2:
                out[rows, :] = y
                continue
            scale = GDN_K_DIM ** -0.5 if part == 0 else 1.0
            for h in range(GDN_HEADS):
                blk = slice(h * GDN_K_DIM, (h + 1) * GDN_K_DIM)
                yh = y[:, blk]
                inv = lax.rsqrt(jnp.sum(yh * yh, axis=-1, keepdims=True) + EPS)
                out[rows, blk] = yh * (inv * scale)
    m = misc_ref[...]
    lane = lax.broadcasted_iota(jnp.int32, m.shape, 1) - GATE_LANE0
    is_gate = (lane >= 0) & (lane < 4 * GDN_HEADS)
    is_a = is_gate & ((lane // GDN_HEADS) % 2 == 0)
    z = m + dtb_ref[...]
    softplus = jnp.maximum(z, 0.0) + jnp.log(1.0 + jnp.exp(-jnp.abs(z)))
    g = -jnp.exp(alog_ref[...]) * softplus
    gate_out[...] = jnp.where(is_a, g, jnp.where(is_gate, _sigmoid(m), 0.0))


def _gdn_prep(gqkv, misc, conv_w, a_log, dt_bias, seq, tm=512):
    t = gqkv.shape[0]
    w3 = 3 * GDN_W
    zeros = jnp.zeros((GDN_HEADS,), F32)
    lane_vals = lambda p: jnp.zeros((1, LANES), F32).at[0, GATE_LANE0:GATE_LANE0 + 4 * GDN_HEADS].set(
        jnp.concatenate([p[0], zeros, p[1], zeros]))
    row = lambda n: pl.BlockSpec((tm, n), lambda i: (i, 0))
    return pl.pallas_call(
        functools.partial(_gdn_prep_kernel, tiles_per_seq=seq // tm),
        grid=(t // tm,),
        in_specs=_halo_specs(tm, GDN_HALO, w3, t) + [
            row(LANES), _resident((GDN_SHORT_CONV, w3)), _resident((1, LANES)), _resident((1, LANES))],
        out_specs=[row(GDN_W), row(GDN_W), row(GDN_W), row(LANES)],
        out_shape=[jax.ShapeDtypeStruct((t, GDN_W), F32)] * 3 + [jax.ShapeDtypeStruct((t, LANES), F32)],
        scratch_shapes=[pltpu.VMEM((tm + 2 * GDN_HALO, w3), F32)],
        compiler_params=_params(("parallel",)),
        name="gdn_prep",
    )(gqkv, gqkv, gqkv, misc, conv_w.reshape(GDN_SHORT_CONV, w3), lane_vals(a_log), lane_vals(dt_bias))


def _bdot(a, b):
    return jnp.dot(a.astype(BF16), b.astype(BF16), preferred_element_type=F32)


def _bdot_nt(a, b):
    return lax.dot_general(a.astype(BF16), b.astype(BF16), (((1,), (1,)), ((), ())),
                           preferred_element_type=F32)


def _gdn_kernel(qf_ref, kf_ref, vf_ref, gf_ref, qb_ref, kb_ref, vb_ref, gb_ref, of_ref, ob_ref,
                state_ref, *, heads):
    c = qf_ref.shape[0]
    d = GDN_K_DIM

    @pl.when(pl.program_id(2) == 0)
    def _():
        state_ref[...] = jnp.zeros_like(state_ref)

    ri = lax.broadcasted_iota(jnp.int32, (c, c), 0)
    ci = lax.broadcasted_iota(jnp.int32, (c, c), 1)
    xor = ri ^ ci
    lane_row = lax.broadcasted_iota(jnp.int32, (1, LANES), 1)
    lane_col = lax.broadcasted_iota(jnp.int32, (LANES, 1), 0)

    chains = []
    for direction, (q_ref, k_ref, v_ref, g_ref, o_ref) in enumerate(
            ((qf_ref, kf_ref, vf_ref, gf_ref, of_ref), (qb_ref, kb_ref, vb_ref, gb_ref, ob_ref))):
        lower = direction == 0
        incl = (ri >= ci) if lower else (ri <= ci)
        strict = (ri > ci) if lower else (ri < ci)
        gates = g_ref[...]
        gc_all = jnp.dot(jnp.where(incl, 1.0, 0.0), gates, preferred_element_type=F32,
                         precision=lax.Precision.HIGHEST)
        gc_all_t = gc_all.T
        last = c - 1 if lower else 0
        for h in range(heads):
            blk = slice(h * d, (h + 1) * d)
            g_lane = GATE_LANE0 + 2 * GDN_HEADS * direction + pl.program_id(1) * heads + h
            gc = jnp.sum(jnp.where(lane_row == g_lane, gc_all, 0.0), axis=1, keepdims=True)
            beta = jnp.sum(jnp.where(lane_row == g_lane + GDN_HEADS, gates, 0.0), axis=1, keepdims=True)
            gc_row = jnp.sum(jnp.where(lane_col == g_lane, gc_all_t, 0.0), axis=0, keepdims=True)
            chains.append(dict(q=q_ref[:, blk], k=k_ref[:, blk], v=v_ref[:, blk], gc=gc, beta=beta,
                               gc_row=gc_row, g_tot=gc[last:last + 1, :], incl=incl, strict=strict,
                               o_ref=o_ref, blk=blk, idx=(direction, h)))

    for ch in chains:
        ch["kk"] = _bdot_nt(ch["k"], ch["k"])
    for ch in chains:
        ch["qk"] = _bdot_nt(ch["q"], ch["k"])
    for ch in chains:
        incl = ch["incl"]
        decay = jnp.where(incl, jnp.exp(jnp.where(incl, ch["gc"] - ch["gc_row"], 0.0)), 0.0)
        ch["a"] = jnp.where(ch["strict"], ch["kk"] * ch["beta"] * decay, 0.0)
        ch["qkd"] = (ch["qk"] * decay).astype(BF16)
        ch["t"] = jnp.where(ri == ci, 1.0, 0.0) - jnp.where(xor < 2, ch["a"], 0.0)
    b = 2
    while b < c:
        level = (xor >= b) & (xor < 2 * b)
        for ch in chains:
            ch["x"] = _bdot(jnp.where(level, ch["a"], 0.0), ch["t"])
        for ch in chains:
            ch["t"] = ch["t"] - _bdot(ch["t"], ch["x"])
        b *= 2
    for ch in chains:
        rhs = jnp.concatenate([ch["v"] * ch["beta"], ch["k"] * (ch["beta"] * jnp.exp(ch["gc"]))], axis=1)
        ch["uw"] = _bdot(ch["t"], rhs).astype(BF16)
    for ch in chains:
        ch["z"] = _bdot(ch["qkd"], ch["uw"])
    for ch in chains:
        k_til = ch["k"] * jnp.exp(ch["g_tot"] - ch["gc"])
        ch["y"] = _bdot(k_til.T, ch["uw"])
    for ch in chains:
        state = state_ref[ch["idx"]]
        q_eff = ch["q"] * jnp.exp(ch["gc"]) - ch["z"][:, d:]
        ch["o_ref"][:, ch["blk"]] = _bdot(q_eff, state) + ch["z"][:, :d]
        state_ref[ch["idx"]] = (state * jnp.exp(ch["g_tot"]) - _bdot(ch["y"][:, d:], state)
                                + ch["y"][:, :d])


def _gdn(q, k, v, gates, batch, seq, heads=GDN_HEADS):
    t = q.shape[0]
    c = GDN_CHUNK
    nc = seq // c
    hw = heads * GDN_K_DIM
    gate_f = pl.BlockSpec((c, LANES), lambda b, hp, i: (b * nc + i, 0))
    gate_b = pl.BlockSpec((c, LANES), lambda b, hp, i: (b * nc + nc - 1 - i, 0))
    head_f = pl.BlockSpec((c, hw), lambda b, hp, i: (b * nc + i, hp))
    head_b = pl.BlockSpec((c, hw), lambda b, hp, i: (b * nc + nc - 1 - i, hp))
    return pl.pallas_call(
        functools.partial(_gdn_kernel, heads=heads),
        grid=(batch, GDN_HEADS // heads, nc),
        in_specs=[head_f, head_f, head_f, gate_f, head_b, head_b, head_b, gate_b],
        out_specs=[head_f, head_b],
        out_shape=[jax.ShapeDtypeStruct((t, GDN_W), F32)] * 2,
        scratch_shapes=[pltpu.VMEM((2, heads, GDN_K_DIM, GDN_V_DIM), F32)],
        compiler_params=_params(("parallel", "parallel", "arbitrary")),
        name="gdn_delta_rule",
    )(q, k, v, gates, q, k, v, gates)


def _out_proj_kernel(x_ref, oa_ref, ob_ref, of_ref, obw_ref, z_ref, onorm_ref, w_ref, g_ref, o_ref):
    na = oa_ref.shape[1]
    nb = ob_ref.shape[1]
    half = x_ref.shape[0] // 2
    for r in (0, half):
        rows = slice(r, r + half)
        oc = of_ref[rows, :] + obw_ref[rows, :]
        z = z_ref[rows, :]
        parts = []
        for h in range(GDN_HEADS):
            blk = slice(h * GDN_V_DIM, (h + 1) * GDN_V_DIM)
            parts.append(_rms(oc[:, blk], onorm_ref[...]) * _silu(z[:, blk]))
        og = jnp.concatenate(parts, axis=1).astype(BF16)
        mix = (jnp.dot(oa_ref[rows, :], w_ref[:na, :], preferred_element_type=F32)
               + jnp.dot(ob_ref[rows, :], w_ref[na:na + nb, :], preferred_element_type=F32)
               + jnp.dot(og, w_ref[na + nb:, :], preferred_element_type=F32))
        o_ref[rows, :] = x_ref[rows, :] + _rms(mix, g_ref[...])


def _out_proj(x, o_a, o_b, o_f, o_bw, z, out_norm, w_out, gain, tm=512):
    t, d = x.shape
    row = lambda n: pl.BlockSpec((tm, n), lambda i: (i, 0))
    return pl.pallas_call(
        _out_proj_kernel,
        grid=(t // tm,),
        in_specs=[row(d), row(o_a.shape[1]), row(o_b.shape[1]), row(GDN_W), row(GDN_W), row(GDN_W),
                  _resident((1, GDN_V_DIM)), _resident(w_out.shape), _resident((1, d))],
        out_specs=row(d),
        out_shape=jax.ShapeDtypeStruct((t, d), F32),
        compiler_params=_params(("parallel",)),
        name="out_proj",
    )(x, o_a, o_b, o_f, o_bw, z, out_norm.reshape(1, -1), w_out, gain.reshape(1, d))


def _ffn_kernel(x_ref, gin_ref, wg_ref, wu_ref, wd_ref, gout_ref, o_ref, hn_ref):
    j = pl.program_id(1)

    @pl.when(j == 0)
    def _():
        hn_ref[...] = _rms(x_ref[...], gin_ref[...]).astype(BF16)
        o_ref[...] = jnp.zeros_like(o_ref)

    hn = hn_ref[...]
    gate = jnp.dot(hn, wg_ref[...], preferred_element_type=F32)
    up = jnp.dot(hn, wu_ref[...], preferred_element_type=F32)
    o_ref[...] += jnp.dot((_silu(gate) * up).astype(BF16), wd_ref[...], preferred_element_type=F32)

    @pl.when(j == pl.num_programs(1) - 1)
    def _():
        o_ref[...] = x_ref[...] + _rms(o_ref[...], gout_ref[...])


def _ffn(x, gain_in, w_gate, w_up, w_down, gain_out, tm=1024, tf=512):
    t, d = x.shape
    f = w_gate.shape[1]
    tm = min(tm, t)
    return pl.pallas_call(
        _ffn_kernel,
        grid=(t // tm, f // tf),
        in_specs=[pl.BlockSpec((tm, d), lambda i, j: (i, 0)),
                  _resident((1, d)),
                  pl.BlockSpec((d, tf), lambda i, j: (0, j)),
                  pl.BlockSpec((d, tf), lambda i, j: (0, j)),
                  pl.BlockSpec((tf, d), lambda i, j: (j, 0)),
                  _resident((1, d))],
        out_specs=pl.BlockSpec((tm, d), lambda i, j: (i, 0)),
        out_shape=jax.ShapeDtypeStruct((t, d), F32),
        scratch_shapes=[pltpu.VMEM((tm, d), BF16)],
        compiler_params=_params(("parallel", "arbitrary"), FFN_VMEM_LIMIT),
        name="swiglu_ffn",
    )(x, gain_in.reshape(1, d), w_gate, w_up, w_down, gain_out.reshape(1, d))


def _cast_kernel(x_ref, o_ref):
    o_ref[...] = x_ref[...].astype(o_ref.dtype)


def _to_bf16(w_stack, layer, rows=256):
    _, r, c = w_stack.shape
    return pl.pallas_call(
        _cast_kernel, grid=(r // rows,),
        in_specs=[pl.BlockSpec((None, rows, c), lambda i: (layer, i, 0))],
        out_specs=pl.BlockSpec((rows, c), lambda i: (i, 0)),
        out_shape=jax.ShapeDtypeStruct((r, c), BF16),
        compiler_params=_params(("parallel",)), name="to_bf16",
    )(w_stack)


def _rot_half_cols(w):
    half = w.shape[-1] // 2
    return jnp.concatenate([-w[..., half:], w[..., :half]], axis=-1)


def _pad_cols(w, n):
    return jnp.pad(w, [(0, 0)] * (w.ndim - 1) + [(0, n - w.shape[-1])])


IN_WIDTHS = (Q_LORA_RANK, KV_LORA_RANK, 2 * CONV_CHANNELS, 3 * GDN_W, GDN_W, LANES, LANES)


def _layout_w_in(w_in):
    sizes = (Q_LORA_RANK, KV_LORA_RANK, QK_ROPE_DIM, 2 * CONV_CHANNELS, 3 * GDN_W, GDN_W, 4 * GDN_HEADS)
    c_q, c_kv, k_rope, conv, gqkv, gz, gates = jnp.split(w_in, np.cumsum(sizes)[:-1].tolist(), axis=1)
    misc = _pad_cols(jnp.concatenate([k_rope, gates], axis=1), LANES)
    krot = _pad_cols(_rot_half_cols(k_rope), LANES)
    return jnp.concatenate([c_q, c_kv, conv, gqkv, gz, misc, krot], axis=1).astype(BF16)


def _layout_w_uq(w_uq):
    r = w_uq.shape[0]
    w = w_uq.reshape(r, MLA_HEADS, QK_HEAD_DIM)
    nope = w[..., :QK_NOPE_DIM]
    rope = w[..., QK_NOPE_DIM:]
    parts = [nope, _pad_cols(rope, LANES), _pad_cols(_rot_half_cols(rope), LANES)]
    return jnp.concatenate([p.reshape(r, MLA_HEADS * LANES) for p in parts], axis=1).astype(BF16)


def _layout_w_ukv(w_ukv):
    r = w_ukv.shape[0]
    w = w_ukv.reshape(r, MLA_HEADS, QK_NOPE_DIM + V_HEAD_DIM)
    return jnp.concatenate([w[..., :QK_NOPE_DIM].reshape(r, -1), w[..., QK_NOPE_DIM:].reshape(r, -1)],
                           axis=1).astype(BF16)


def _layer(x, cos, sin, batch, seq, pre_mix_norm, w_in, q_a_norm, w_uq, kv_a_norm, w_ukv, conv_dw_w,
           conv_dw_b, conv_ln_g, conv_ln_b, gdn_conv_w, gdn_a_log, gdn_dt_bias, gdn_out_norm, w_out,
           post_mix_norm, pre_ffn_norm, w_gate, w_up, w_down, post_ffn_norm):
    cq, ckv, conv_in, gqkv, gz, misc, krot = _in_proj(x, pre_mix_norm, _layout_w_in(w_in), IN_WIDTHS)
    q_scale = QK_HEAD_DIM ** -0.5 * float(np.log2(np.e))
    q, k_nope, k_rope, v = _mla_prep(cq, ckv, misc, krot, cos, sin, q_a_norm, kv_a_norm,
                                     _layout_w_uq(w_uq), _layout_w_ukv(w_ukv), q_scale)
    o_a = _attention(q, k_nope, k_rope, v, batch, seq)
    o_b = _conformer(conv_in, conv_dw_w, conv_dw_b, conv_ln_g, conv_ln_b, seq)
    gq, gk, gv, gates = _gdn_prep(gqkv, misc, gdn_conv_w, gdn_a_log, gdn_dt_bias, seq)
    o_f, o_bw = _gdn(gq, gk, gv, gates, batch, seq)
    x = _out_proj(x, o_a, o_b, o_f, o_bw, gz, gdn_out_norm, w_out, post_mix_norm)
    return _ffn(x, pre_ffn_norm, w_gate, w_up, w_down, post_ffn_norm)


def kernel(x, positions, pre_mix_norm, w_in, q_a_norm, w_uq, kv_a_norm, w_ukv, conv_dw_w, conv_dw_b,
           conv_ln_g, conv_ln_b, gdn_conv_w, gdn_a_log, gdn_dt_bias, gdn_out_norm, w_out, post_mix_norm,
           pre_ffn_norm, w_gate, w_up, w_down, post_ffn_norm):
    batch, seq, d = x.shape
    cos, sin = _rope_tables(positions)
    h = x.reshape(batch * seq, d)
    per_layer = (pre_mix_norm, w_in, q_a_norm, w_uq, kv_a_norm, w_ukv, conv_dw_w, conv_dw_b, conv_ln_g,
                 conv_ln_b, gdn_conv_w, gdn_a_log, gdn_dt_bias, gdn_out_norm, w_out, post_mix_norm,
                 pre_ffn_norm, w_gate, w_up, w_down, post_ffn_norm)
    big = (w_out, w_gate, w_up, w_down)
    for l in range(pre_mix_norm.shape[0]):
        h = _layer(h, cos, sin, batch, seq,
                   *(_to_bf16(p, l) if any(p is b for b in big) else p[l] for p in per_layer))
    return h.reshape(batch, seq, d)
```

```python
import functools

import numpy as np
import jax
import jax.numpy as jnp
from jax import lax
from jax.experimental import pallas as pl
from jax.experimental.pallas import tpu as pltpu

F32 = jnp.float32
BF16 = jnp.bfloat16

EPS = 1e-6
LANES = 128
SUBLANES = 8
VMEM_LIMIT = 56 * 1024 * 1024
FFN_VMEM_LIMIT = 62 * 1024 * 1024

MLA_HEADS = 8
Q_LORA_RANK = 512
KV_LORA_RANK = 512
QK_NOPE_DIM = 128
QK_ROPE_DIM = 64
QK_HEAD_DIM = QK_NOPE_DIM + QK_ROPE_DIM
V_HEAD_DIM = 128
ROPE_THETA = 10000.0
QK_PAD_DIM = 2 * LANES
V_PAD_DIM = 2 * LANES
CONV_CHANNELS = 512
CONV_WIDTH = 31
GDN_HEADS = 4
GDN_K_DIM = 128
GDN_V_DIM = 128
GDN_SHORT_CONV = 5
GDN_W = GDN_HEADS * GDN_K_DIM
GDN_CHUNK = 256
GATE_LANE0 = QK_ROPE_DIM

def _params(semantics, vmem_limit=VMEM_LIMIT):
    return pltpu.CompilerParams(dimension_semantics=semantics, vmem_limit_bytes=vmem_limit)


def _resident(shape):
    return pl.BlockSpec(shape, lambda *_: (0,) * len(shape), pipeline_mode=pl.Buffered(1))


def _resident_layer(shape, layer):
    return pl.BlockSpec((None,) + tuple(shape), lambda *_: (layer,) + (0,) * len(shape),
                        pipeline_mode=pl.Buffered(1))


def _rms(x, w):
    return x * lax.rsqrt(jnp.mean(x * x, axis=-1, keepdims=True) + EPS) * w


def _sigmoid(x):
    return 1.0 / (1.0 + jnp.exp(-x))


def _silu(x):
    return x * _sigmoid(x)


def _rope_table_kernel(pos_ref, inv_ref, cos_ref, sin_ref):
    ang = pos_ref[...].astype(F32) * inv_ref[...]
    lane = lax.broadcasted_iota(jnp.int32, ang.shape, 1)
    valid = lane < QK_ROPE_DIM
    cos_ref[...] = jnp.where(valid, jnp.cos(ang), 0.0)
    sin_ref[...] = jnp.where(valid, jnp.sin(ang), 0.0)


def _rope_tables(positions):
    t = positions.size
    tm = min(t, 2048)
    half = QK_ROPE_DIM // 2
    inv = 1.0 / (ROPE_THETA ** (jnp.arange(0, QK_ROPE_DIM, 2, dtype=F32) / QK_ROPE_DIM))
    inv_row = jnp.zeros((1, LANES), F32).at[0, :half].set(inv).at[0, half:2 * half].set(inv)
    spec = pl.BlockSpec((tm, LANES), lambda i: (i, 0))
    return pl.pallas_call(
        _rope_table_kernel,
        grid=(t // tm,),
        in_specs=[pl.BlockSpec((tm, 1), lambda i: (i, 0)), _resident((1, LANES))],
        out_specs=[spec, spec],
        out_shape=[jax.ShapeDtypeStruct((t, LANES), F32)] * 2,
        compiler_params=_params(("parallel",)),
        name="rope_tables",
    )(positions.reshape(t, 1), inv_row)


def _in_proj_kernel(x_ref, g_ref, w_ref, *out_refs):
    half = x_ref.shape[0] // 2
    for r in (0, half):
        rows = slice(r, r + half)
        xn = _rms(x_ref[rows, :], g_ref[...]).astype(BF16)
        off = 0
        for o_ref in out_refs:
            n = o_ref.shape[1]
            o_ref[rows, :] = jnp.dot(xn, w_ref[:, off:off + n],
                                     preferred_element_type=F32).astype(o_ref.dtype)
            off += n


def _in_proj(x, gain, w_stack, layer, widths, tm=512):
    t, d = x.shape
    n = w_stack.shape[2]
    assert sum(widths) == n
    return pl.pallas_call(
        _in_proj_kernel,
        grid=(t // tm,),
        in_specs=[pl.BlockSpec((tm, d), lambda i: (i, 0)), _resident((1, d)),
                  _resident_layer((d, n), layer)],
        out_specs=[pl.BlockSpec((tm, wd), lambda i: (i, 0)) for wd in widths],
        out_shape=[jax.ShapeDtypeStruct((t, wd), F32) for wd in widths],
        compiler_params=_params(("parallel",)),
        name="in_proj",
    )(x, gain.reshape(1, d), w_stack)


def _mla_prep_kernel(cq_ref, ckv_ref, misc_ref, krot_ref, cos_ref, sin_ref, qn_ref, kvn_ref,
                     wq_ref, wkv_ref, q_out, kn_out, kr_out, v_out, *, q_scale):
    cos = cos_ref[...]
    sin = sin_ref[...]
    hw = MLA_HEADS * LANES
    cqn = _rms(cq_ref[...], qn_ref[...]).astype(BF16)
    q_nope = jnp.dot(cqn, wq_ref[:, :hw], preferred_element_type=F32)
    q_rope = jnp.dot(cqn, wq_ref[:, hw:2 * hw], preferred_element_type=F32)
    q_rot = jnp.dot(cqn, wq_ref[:, 2 * hw:], preferred_element_type=F32)
    ckvn = _rms(ckv_ref[...], kvn_ref[...]).astype(BF16)
    kn_out[...] = jnp.dot(ckvn, wkv_ref[:, :hw], preferred_element_type=F32).astype(kn_out.dtype)
    v_out[...] = jnp.dot(ckvn, wkv_ref[:, hw:], preferred_element_type=F32).astype(v_out.dtype)
    kr_out[...] = (misc_ref[...] * cos + krot_ref[...] * sin).astype(kr_out.dtype)
    for h in range(MLA_HEADS):
        blk = slice(h * LANES, (h + 1) * LANES)
        lo = slice(h * QK_PAD_DIM, h * QK_PAD_DIM + LANES)
        hi = slice(h * QK_PAD_DIM + LANES, (h + 1) * QK_PAD_DIM)
        q_out[:, lo] = (q_nope[:, blk] * q_scale).astype(q_out.dtype)
        q_out[:, hi] = ((q_rope[:, blk] * cos + q_rot[:, blk] * sin) * q_scale).astype(q_out.dtype)


def _mla_prep(cq, ckv, misc, krot, cos, sin, q_a_norm, kv_a_norm, wq_stack, wkv_stack, layer, q_scale,
              tm=512):
    t = cq.shape[0]
    row = lambda n: pl.BlockSpec((tm, n), lambda i: (i, 0))
    q_w = MLA_HEADS * QK_PAD_DIM
    kn_w = MLA_HEADS * QK_NOPE_DIM
    v_w = MLA_HEADS * V_HEAD_DIM
    bf = lambda n: jax.ShapeDtypeStruct((t, n), BF16)
    return pl.pallas_call(
        functools.partial(_mla_prep_kernel, q_scale=q_scale),
        grid=(t // tm,),
        in_specs=[row(Q_LORA_RANK), row(KV_LORA_RANK), row(LANES), row(LANES), row(LANES), row(LANES),
                  _resident((1, Q_LORA_RANK)), _resident((1, KV_LORA_RANK)),
                  _resident_layer(wq_stack.shape[1:], layer), _resident_layer(wkv_stack.shape[1:], layer)],
        out_specs=[row(q_w), row(kn_w), row(LANES), row(v_w)],
        out_shape=[bf(q_w), bf(kn_w), bf(LANES), bf(v_w)],
        compiler_params=_params(("parallel",)),
        name="mla_prep",
    )(cq, ckv, misc, krot, cos, sin, q_a_norm.reshape(1, -1), kv_a_norm.reshape(1, -1),
      wq_stack, wkv_stack)


ATTN_TK = 512


def _attn_kernel(q_ref, kn_ref, kr_ref, v_ref, o_ref, *, heads):
    tq = q_ref.shape[0]
    m = [jnp.full((tq, 1), -jnp.inf, F32)] * heads
    acc = [jnp.zeros((tq, V_PAD_DIM), F32)] * heads
    ones_col = jnp.where(lax.broadcasted_iota(jnp.int32, (ATTN_TK, LANES), 1) == 0, 1.0, 0.0).astype(BF16)
    for j in range(kn_ref.shape[0] // ATTN_TK):
        rows = slice(j * ATTN_TK, (j + 1) * ATTN_TK)
        k_rope = kr_ref[rows, :]
        for h in range(heads):
            head = slice(h * LANES, (h + 1) * LANES)
            k_op = jnp.concatenate([kn_ref[rows, head], k_rope], axis=1)
            v_op = jnp.concatenate([v_ref[rows, head], ones_col], axis=1)
            s = lax.dot_general(q_ref[:, h * QK_PAD_DIM:(h + 1) * QK_PAD_DIM], k_op,
                                (((1,), (1,)), ((), ())), preferred_element_type=F32)
            m_new = jnp.maximum(m[h], jnp.max(s, axis=-1, keepdims=True))
            p = jnp.exp2(s - m_new).astype(BF16)
            acc[h] = acc[h] * jnp.exp2(m[h] - m_new) + jnp.dot(p, v_op, preferred_element_type=F32)
            m[h] = m_new
    for h in range(heads):
        o_ref[:, h * V_HEAD_DIM:(h + 1) * V_HEAD_DIM] = (
            acc[h][:, :V_HEAD_DIM] / acc[h][:, V_HEAD_DIM:V_HEAD_DIM + 1]).astype(o_ref.dtype)


def _attention(q, k_nope, k_rope, v, batch, seq, tq=1024, heads=4):
    t = q.shape[0]
    nq = seq // tq
    return pl.pallas_call(
        functools.partial(_attn_kernel, heads=heads),
        grid=(batch, MLA_HEADS // heads, nq),
        in_specs=[pl.BlockSpec((tq, heads * QK_PAD_DIM), lambda b, h, i: (b * nq + i, h)),
                  pl.BlockSpec((seq, heads * QK_NOPE_DIM), lambda b, h, i: (b, h)),
                  pl.BlockSpec((seq, LANES), lambda b, h, i: (b, 0)),
                  pl.BlockSpec((seq, heads * V_HEAD_DIM), lambda b, h, i: (b, h))],
        out_specs=pl.BlockSpec((tq, heads * V_HEAD_DIM), lambda b, h, i: (b * nq + i, h)),
        out_shape=jax.ShapeDtypeStruct((t, MLA_HEADS * V_HEAD_DIM), BF16),
        compiler_params=_params(("parallel", "parallel", "arbitrary")),
        name="mla_attention",
    )(q, k_nope, k_rope, v)


def _halo_specs(tm, halo, width, n_rows):
    per = tm // halo
    last = n_rows // halo - 1
    return [pl.BlockSpec((tm, width), lambda i: (i, 0)),
            pl.BlockSpec((halo, width), lambda i: (jnp.maximum(i * per - 1, 0), 0)),
            pl.BlockSpec((halo, width), lambda i: (jnp.minimum((i + 1) * per, last), 0))]


def _seq_edges(tiles_per_seq):
    i = pl.program_id(0) % tiles_per_seq
    return i == 0, i == tiles_per_seq - 1


CONF_HALO = 16
CONF_ROWS = 32


def _conformer_kernel(cur_ref, prev_ref, next_ref, w_ref, b_ref, g_ref, beta_ref, o_ref, ext_ref,
                      shift_ref, *, tiles_per_seq):
    tm = cur_ref.shape[0]
    c = CONV_CHANNELS
    first, last = _seq_edges(tiles_per_seq)

    def glu(u):
        return u[:, :c] * _sigmoid(u[:, c:])

    ext_ref[0:CONF_HALO, :] = jnp.where(first, 0.0, glu(prev_ref[...]))
    ext_ref[CONF_HALO:CONF_HALO + tm, :] = glu(cur_ref[...])
    ext_ref[CONF_HALO + tm:, :] = jnp.where(last, 0.0, glu(next_ref[...]))
    n_shift = shift_ref.shape[1]
    for s in range(1, SUBLANES):
        shift_ref[s, :, :] = ext_ref[s:s + n_shift, :]
    pad = CONV_WIDTH // 2
    bias = b_ref[...]
    for r in range(0, tm, CONF_ROWS):
        acc = jnp.broadcast_to(bias, (CONF_ROWS, c))
        for d in range(CONV_WIDTH):
            start = CONF_HALO + r - pad + d
            s = start % SUBLANES
            base = start - s
            win = ext_ref[base:base + CONF_ROWS, :] if s == 0 else shift_ref[s, base:base + CONF_ROWS, :]
            acc = acc + win * w_ref[d:d + 1, :]
        mu = jnp.mean(acc, axis=-1, keepdims=True)
        cen = acc - mu
        var = jnp.mean(cen * cen, axis=-1, keepdims=True)
        y = cen * lax.rsqrt(var + EPS) * g_ref[...] + beta_ref[...]
        o_ref[r:r + CONF_ROWS, :] = _silu(y).astype(o_ref.dtype)


def _conformer(conv_in, dw_w, dw_b, ln_g, ln_b, seq, tm=512):
    t = conv_in.shape[0]
    c = CONV_CHANNELS
    return pl.pallas_call(
        functools.partial(_conformer_kernel, tiles_per_seq=seq // tm),
        grid=(t // tm,),
        in_specs=_halo_specs(tm, CONF_HALO, 2 * c, t) + [
            _resident((CONV_WIDTH, c)), _resident((1, c)), _resident((1, c)), _resident((1, c))],
        out_specs=pl.BlockSpec((tm, c), lambda i: (i, 0)),
        out_shape=jax.ShapeDtypeStruct((t, c), BF16),
        scratch_shapes=[pltpu.VMEM((tm + 2 * CONF_HALO, c), F32),
                        pltpu.VMEM((SUBLANES, tm + 2 * CONF_HALO - SUBLANES, c), F32)],
        compiler_params=_params(("parallel",)),
        name="conformer_conv",
    )(conv_in, conv_in, conv_in, dw_w.reshape(CONV_WIDTH, c), dw_b.reshape(1, c),
      ln_g.reshape(1, c), ln_b.reshape(1, c))


GDN_HALO = 8
GDN_ROWS = 64


def _gdn_prep_kernel(cur_ref, prev_ref, next_ref, misc_ref, w_ref, alog_ref, dtb_ref,
                     q_out, k_out, v_out, gate_out, ext_ref, *, tiles_per_seq):
    tm = cur_ref.shape[0]
    first, last = _seq_edges(tiles_per_seq)
    ext_ref[0:GDN_HALO, :] = jnp.where(first, 0.0, prev_ref[...])
    ext_ref[GDN_HALO:GDN_HALO + tm, :] = cur_ref[...]
    ext_ref[GDN_HALO + tm:, :] = jnp.where(last, 0.0, next_ref[...])
    pad = GDN_SHORT_CONV // 2
    for r in range(0, tm, GDN_ROWS):
        rows = slice(r, r + GDN_ROWS)
        for part, out in enumerate((q_out, k_out, v_out)):
            cols = slice(part * GDN_W, (part + 1) * GDN_W)
            acc = jnp.zeros((GDN_ROWS, GDN_W), F32)
            for d in range(GDN_SHORT_CONV):
                start = GDN_HALO + r - pad + d
                acc = acc + ext_ref[start:start + GDN_ROWS, cols] * w_ref[d:d + 1, cols]
            y = _silu(acc)
            if part == 2:
                out[rows, :] = y
                continue
            scale = GDN_K_DIM ** -0.5 if part == 0 else 1.0
            for h in range(GDN_HEADS):
                blk = slice(h * GDN_K_DIM, (h + 1) * GDN_K_DIM)
                yh = y[:, blk]
                inv = lax.rsqrt(jnp.sum(yh * yh, axis=-1, keepdims=True) + EPS)
                out[rows, blk] = yh * (inv * scale)
    m = misc_ref[...]
    lane = lax.broadcasted_iota(jnp.int32, m.shape, 1) - GATE_LANE0
    is_gate = (lane >= 0) & (lane < 4 * GDN_HEADS)
    is_a = is_gate & ((lane // GDN_HEADS) % 2 == 0)
    z = m + dtb_ref[...]
    softplus = jnp.maximum(z, 0.0) + jnp.log(1.0 + jnp.exp(-jnp.abs(z)))
    g = -jnp.exp(alog_ref[...]) * softplus
    gate_out[...] = jnp.where(is_a, g, jnp.where(is_gate, _sigmoid(m), 0.0))


def _gdn_prep(gqkv, misc, conv_w, a_log, dt_bias, seq, tm=512):
    t = gqkv.shape[0]
    w3 = 3 * GDN_W
    zeros = jnp.zeros((GDN_HEADS,), F32)
    lane_vals = lambda p: jnp.zeros((1, LANES), F32).at[0, GATE_LANE0:GATE_LANE0 + 4 * GDN_HEADS].set(
        jnp.concatenate([p[0], zeros, p[1], zeros]))
    row = lambda n: pl.BlockSpec((tm, n), lambda i: (i, 0))
    return pl.pallas_call(
        functools.partial(_gdn_prep_kernel, tiles_per_seq=seq // tm),
        grid=(t // tm,),
        in_specs=_halo_specs(tm, GDN_HALO, w3, t) + [
            row(LANES), _resident((GDN_SHORT_CONV, w3)), _resident((1, LANES)), _resident((1, LANES))],
        out_specs=[row(GDN_W), row(GDN_W), row(GDN_W), row(LANES)],
        out_shape=[jax.ShapeDtypeStruct((t, GDN_W), F32)] * 3 + [jax.ShapeDtypeStruct((t, LANES), F32)],
        scratch_shapes=[pltpu.VMEM((tm + 2 * GDN_HALO, w3), F32)],
        compiler_params=_params(("parallel",)),
        name="gdn_prep",
    )(gqkv, gqkv, gqkv, misc, conv_w.reshape(GDN_SHORT_CONV, w3), lane_vals(a_log), lane_vals(dt_bias))


def _bdot(a, b):
    return jnp.dot(a.astype(BF16), b.astype(BF16), preferred_element_type=F32)


def _bdot_nt(a, b):
    return lax.dot_general(a.astype(BF16), b.astype(BF16), (((1,), (1,)), ((), ())),
                           preferred_element_type=F32)


def _gdn_kernel(qf_ref, kf_ref, vf_ref, gf_ref, qb_ref, kb_ref, vb_ref, gb_ref, of_ref, ob_ref,
                state_ref, *, heads):
    c = qf_ref.shape[0]
    d = GDN_K_DIM

    @pl.when(pl.program_id(2) == 0)
    def _():
        state_ref[...] = jnp.zeros_like(state_ref)

    ri = lax.broadcasted_iota(jnp.int32, (c, c), 0)
    ci = lax.broadcasted_iota(jnp.int32, (c, c), 1)
    xor = ri ^ ci
    lane_row = lax.broadcasted_iota(jnp.int32, (1, LANES), 1)
    lane_col = lax.broadcasted_iota(jnp.int32, (LANES, 1), 0)

    chains = []
    for direction, (q_ref, k_ref, v_ref, g_ref, o_ref) in enumerate(
            ((qf_ref, kf_ref, vf_ref, gf_ref, of_ref), (qb_ref, kb_ref, vb_ref, gb_ref, ob_ref))):
        lower = direction == 0
        incl = (ri >= ci) if lower else (ri <= ci)
        strict = (ri > ci) if lower else (ri < ci)
        gates = g_ref[...]
        gc_all = jnp.dot(jnp.where(incl, 1.0, 0.0), gates, preferred_element_type=F32,
                         precision=lax.Precision.HIGHEST)
        gc_all_t = gc_all.T
        last = c - 1 if lower else 0
        for h in range(heads):
            blk = slice(h * d, (h + 1) * d)
            g_lane = GATE_LANE0 + 2 * GDN_HEADS * direction + pl.program_id(1) * heads + h
            gc = jnp.sum(jnp.where(lane_row == g_lane, gc_all, 0.0), axis=1, keepdims=True)
            beta = jnp.sum(jnp.where(lane_row == g_lane + GDN_HEADS, gates, 0.0), axis=1, keepdims=True)
            gc_row = jnp.sum(jnp.where(lane_col == g_lane, gc_all_t, 0.0), axis=0, keepdims=True)
            chains.append(dict(q=q_ref[:, blk], k=k_ref[:, blk], v=v_ref[:, blk], gc=gc, beta=beta,
                               gc_row=gc_row, g_tot=gc[last:last + 1, :], incl=incl, strict=strict,
                               o_ref=o_ref, blk=blk, idx=(direction, h)))

    for ch in chains:
        ch["kk"] = _bdot_nt(ch["k"], ch["k"])
    for ch in chains:
        ch["qk"] = _bdot_nt(ch["q"], ch["k"])
    for ch in chains:
        incl = ch["incl"]
        decay = jnp.where(incl, jnp.exp(jnp.where(incl, ch["gc"] - ch["gc_row"], 0.0)), 0.0)
        ch["a"] = jnp.where(ch["strict"], ch["kk"] * ch["beta"] * decay, 0.0)
        ch["qkd"] = (ch["qk"] * decay).astype(BF16)
        ch["t"] = jnp.where(ri == ci, 1.0, 0.0) - jnp.where(xor < 2, ch["a"], 0.0)
    b = 2
    while b < c:
        level = (xor >= b) & (xor < 2 * b)
        for ch in chains:
            ch["x"] = _bdot(jnp.where(level, ch["a"], 0.0), ch["t"])
        for ch in chains:
            ch["t"] = ch["t"] - _bdot(ch["t"], ch["x"])
        b *= 2
    for ch in chains:
        rhs = jnp.concatenate([ch["v"] * ch["beta"], ch["k"] * (ch["beta"] * jnp.exp(ch["gc"]))], axis=1)
        ch["uw"] = _bdot(ch["t"], rhs).astype(BF16)
    for ch in chains:
        ch["z"] = _bdot(ch["qkd"], ch["uw"])
    for ch in chains:
        k_til = ch["k"] * jnp.exp(ch["g_tot"] - ch["gc"])
        ch["y"] = _bdot(k_til.T, ch["uw"])
    for ch in chains:
        state = state_ref[ch["idx"]]
        q_eff = ch["q"] * jnp.exp(ch["gc"]) - ch["z"][:, d:]
        ch["o_ref"][:, ch["blk"]] = _bdot(q_eff, state) + ch["z"][:, :d]
        state_ref[ch["idx"]] = (state * jnp.exp(ch["g_tot"]) - _bdot(ch["y"][:, d:], state)
                                + ch["y"][:, :d])


def _gdn(q, k, v, gates, batch, seq, heads=GDN_HEADS):
    t = q.shape[0]
    c = GDN_CHUNK
    nc = seq // c
    hw = heads * GDN_K_DIM
    gate_f = pl.BlockSpec((c, LANES), lambda b, hp, i: (b * nc + i, 0))
    gate_b = pl.BlockSpec((c, LANES), lambda b, hp, i: (b * nc + nc - 1 - i, 0))
    head_f = pl.BlockSpec((c, hw), lambda b, hp, i: (b * nc + i, hp))
    head_b = pl.BlockSpec((c, hw), lambda b, hp, i: (b * nc + nc - 1 - i, hp))
    return pl.pallas_call(
        functools.partial(_gdn_kernel, heads=heads),
        grid=(batch, GDN_HEADS // heads, nc),
        in_specs=[head_f, head_f, head_f, gate_f, head_b, head_b, head_b, gate_b],
        out_specs=[head_f, head_b],
        out_shape=[jax.ShapeDtypeStruct((t, GDN_W), F32)] * 2,
        scratch_shapes=[pltpu.VMEM((2, heads, GDN_K_DIM, GDN_V_DIM), F32)],
        compiler_params=_params(("parallel", "parallel", "arbitrary")),
        name="gdn_delta_rule",
    )(q, k, v, gates, q, k, v, gates)


def _out_proj_kernel(x_ref, oa_ref, ob_ref, of_ref, obw_ref, z_ref, onorm_ref, w_ref, g_ref, o_ref):
    na = oa_ref.shape[1]
    nb = ob_ref.shape[1]
    half = x_ref.shape[0] // 2
    for r in (0, half):
        rows = slice(r, r + half)
        oc = of_ref[rows, :] + obw_ref[rows, :]
        z = z_ref[rows, :]
        parts = []
        for h in range(GDN_HEADS):
            blk = slice(h * GDN_V_DIM, (h + 1) * GDN_V_DIM)
            parts.append(_rms(oc[:, blk], onorm_ref[...]) * _silu(z[:, blk]))
        og = jnp.concatenate(parts, axis=1).astype(BF16)
        mix = (jnp.dot(oa_ref[rows, :], w_ref[:na, :], preferred_element_type=F32)
               + jnp.dot(ob_ref[rows, :], w_ref[na:na + nb, :], preferred_element_type=F32)
               + jnp.dot(og, w_ref[na + nb:, :], preferred_element_type=F32))
        o_ref[rows, :] = x_ref[rows, :] + _rms(mix, g_ref[...])


def _out_proj(x, o_a, o_b, o_f, o_bw, z, out_norm, w_out, gain, tm=512):
    t, d = x.shape
    row = lambda n: pl.BlockSpec((tm, n), lambda i: (i, 0))
    return pl.pallas_call(
        _out_proj_kernel,
        grid=(t // tm,),
        in_specs=[row(d), row(o_a.shape[1]), row(o_b.shape[1]), row(GDN_W), row(GDN_W), row(GDN_W),
                  _resident((1, GDN_V_DIM)), _resident(w_out.shape), _resident((1, d))],
        out_specs=row(d),
        out_shape=jax.ShapeDtypeStruct((t, d), F32),
        compiler_params=_params(("parallel",)),
        name="out_proj",
    )(x, o_a, o_b, o_f, o_bw, z, out_norm.reshape(1, -1), w_out, gain.reshape(1, d))


def _ffn_kernel(x_ref, gin_ref, wg_ref, wu_ref, wd_ref, gout_ref, o_ref, hn_ref):
    j = pl.program_id(1)

    @pl.when(j == 0)
    def _():
        hn_ref[...] = _rms(x_ref[...], gin_ref[...]).astype(BF16)
        o_ref[...] = jnp.zeros_like(o_ref)

    hn = hn_ref[...]
    gate = jnp.dot(hn, wg_ref[...], preferred_element_type=F32)
    up = jnp.dot(hn, wu_ref[...], preferred_element_type=F32)
    o_ref[...] += jnp.dot((_silu(gate) * up).astype(BF16), wd_ref[...], preferred_element_type=F32)

    @pl.when(j == pl.num_programs(1) - 1)
    def _():
        o_ref[...] = x_ref[...] + _rms(o_ref[...], gout_ref[...])


def _ffn(x, gain_in, w_gate, w_up, w_down, gain_out, tm=1024, tf=512):
    t, d = x.shape
    f = w_gate.shape[1]
    tm = min(tm, t)
    return pl.pallas_call(
        _ffn_kernel,
        grid=(t // tm, f // tf),
        in_specs=[pl.BlockSpec((tm, d), lambda i, j: (i, 0)),
                  _resident((1, d)),
                  pl.BlockSpec((d, tf), lambda i, j: (0, j)),
                  pl.BlockSpec((d, tf), lambda i, j: (0, j)),
                  pl.BlockSpec((tf, d), lambda i, j: (j, 0)),
                  _resident((1, d))],
        out_specs=pl.BlockSpec((tm, d), lambda i, j: (i, 0)),
        out_shape=jax.ShapeDtypeStruct((t, d), F32),
        scratch_shapes=[pltpu.VMEM((tm, d), BF16)],
        compiler_params=_params(("parallel", "arbitrary"), FFN_VMEM_LIMIT),
        name="swiglu_ffn",
    )(x, gain_in.reshape(1, d), w_gate, w_up, w_down, gain_out.reshape(1, d))


def _cast_kernel(x_ref, o_ref):
    o_ref[...] = x_ref[...].astype(o_ref.dtype)


def _to_bf16(w_stack, layer, rows=256):
    _, r, c = w_stack.shape
    return pl.pallas_call(
        _cast_kernel, grid=(r // rows,),
        in_specs=[pl.BlockSpec((None, rows, c), lambda i: (layer, i, 0))],
        out_specs=pl.BlockSpec((rows, c), lambda i: (i, 0)),
        out_shape=jax.ShapeDtypeStruct((r, c), BF16),
        compiler_params=_params(("parallel",)), name="to_bf16",
    )(w_stack)


def _rot_half_cols(w):
    half = w.shape[-1] // 2
    return jnp.concatenate([-w[..., half:], w[..., :half]], axis=-1)


def _pad_cols(w, n):
    return jnp.pad(w, [(0, 0)] * (w.ndim - 1) + [(0, n - w.shape[-1])])


IN_WIDTHS = (Q_LORA_RANK, KV_LORA_RANK, 2 * CONV_CHANNELS, 3 * GDN_W, GDN_W, LANES, LANES)


def _layout_w_in(w_in):
    sizes = (Q_LORA_RANK, KV_LORA_RANK, QK_ROPE_DIM, 2 * CONV_CHANNELS, 3 * GDN_W, GDN_W, 4 * GDN_HEADS)
    c_q, c_kv, k_rope, conv, gqkv, gz, gates = jnp.split(w_in, np.cumsum(sizes)[:-1].tolist(), axis=-1)
    misc = _pad_cols(jnp.concatenate([k_rope, gates], axis=-1), LANES)
    krot = _pad_cols(_rot_half_cols(k_rope), LANES)
    return jnp.concatenate([c_q, c_kv, conv, gqkv, gz, misc, krot], axis=-1).astype(BF16)


def _layout_w_uq(w_uq):
    lead = w_uq.shape[:-1]
    w = w_uq.reshape(*lead, MLA_HEADS, QK_HEAD_DIM)
    nope = w[..., :QK_NOPE_DIM]
    rope = w[..., QK_NOPE_DIM:]
    parts = [nope, _pad_cols(rope, LANES), _pad_cols(_rot_half_cols(rope), LANES)]
    return jnp.concatenate([p.reshape(*lead, MLA_HEADS * LANES) for p in parts], axis=-1).astype(BF16)


def _layout_w_ukv(w_ukv):
    lead = w_ukv.shape[:-1]
    w = w_ukv.reshape(*lead, MLA_HEADS, QK_NOPE_DIM + V_HEAD_DIM)
    return jnp.concatenate([w[..., :QK_NOPE_DIM].reshape(*lead, -1), w[..., QK_NOPE_DIM:].reshape(*lead, -1)],
                           axis=-1).astype(BF16)


def _layer(x, cos, sin, batch, seq, layer, w_in_stack, w_uq_stack, w_ukv_stack, pre_mix_norm, q_a_norm,
           kv_a_norm, conv_dw_w, conv_dw_b, conv_ln_g, conv_ln_b, gdn_conv_w, gdn_a_log, gdn_dt_bias,
           gdn_out_norm, w_out, post_mix_norm, pre_ffn_norm, w_gate, w_up, w_down, post_ffn_norm):
    cq, ckv, conv_in, gqkv, gz, misc, krot = _in_proj(x, pre_mix_norm, w_in_stack, layer, IN_WIDTHS)
    q_scale = QK_HEAD_DIM ** -0.5 * float(np.log2(np.e))
    q, k_nope, k_rope, v = _mla_prep(cq, ckv, misc, krot, cos, sin, q_a_norm, kv_a_norm,
                                     w_uq_stack, w_ukv_stack, layer, q_scale)
    o_a = _attention(q, k_nope, k_rope, v, batch, seq)
    o_b = _conformer(conv_in, conv_dw_w, conv_dw_b, conv_ln_g, conv_ln_b, seq)
    gq, gk, gv, gates = _gdn_prep(gqkv, misc, gdn_conv_w, gdn_a_log, gdn_dt_bias, seq)
    o_f, o_bw = _gdn(gq, gk, gv, gates, batch, seq)
    x = _out_proj(x, o_a, o_b, o_f, o_bw, gz, gdn_out_norm, w_out, post_mix_norm)
    return _ffn(x, pre_ffn_norm, w_gate, w_up, w_down, post_ffn_norm)


def kernel(x, positions, pre_mix_norm, w_in, q_a_norm, w_uq, kv_a_norm, w_ukv, conv_dw_w, conv_dw_b,
           conv_ln_g, conv_ln_b, gdn_conv_w, gdn_a_log, gdn_dt_bias, gdn_out_norm, w_out, post_mix_norm,
           pre_ffn_norm, w_gate, w_up, w_down, post_ffn_norm):
    batch, seq, d = x.shape
    cos, sin = _rope_tables(positions)
    h = x.reshape(batch * seq, d)
    stacks = (_layout_w_in(w_in), _layout_w_uq(w_uq), _layout_w_ukv(w_ukv))
    per_layer = (pre_mix_norm, q_a_norm, kv_a_norm, conv_dw_w, conv_dw_b, conv_ln_g, conv_ln_b,
                 gdn_conv_w, gdn_a_log, gdn_dt_bias, gdn_out_norm, w_out, post_mix_norm, pre_ffn_norm,
                 w_gate, w_up, w_down, post_ffn_norm)
    big = (w_out, w_gate, w_up, w_down)
    for l in range(pre_mix_norm.shape[0]):
        h = _layer(h, cos, sin, batch, seq, l, *stacks,
                   *(_to_bf16(p, l) if any(p is b for b in big) else p[l] for p in per_layer))
    return h.reshape(batch, seq, d)
```

```python
import functools

import numpy as np
import jax
import jax.numpy as jnp
from jax import lax
from jax.experimental import pallas as pl
from jax.experimental.pallas import tpu as pltpu

F32 = jnp.float32
BF16 = jnp.bfloat16

EPS = 1e-6
LANES = 128
SUBLANES = 8
VMEM_LIMIT = 56 * 1024 * 1024
FFN_VMEM_LIMIT = 62 * 1024 * 1024

MLA_HEADS = 8
Q_LORA_RANK = 512
KV_LORA_RANK = 512
QK_NOPE_DIM = 128
QK_ROPE_DIM = 64
QK_HEAD_DIM = QK_NOPE_DIM + QK_ROPE_DIM
V_HEAD_DIM = 128
ROPE_THETA = 10000.0
QK_PAD_DIM = 2 * LANES
V_PAD_DIM = 2 * LANES
CONV_CHANNELS = 512
CONV_WIDTH = 31
GDN_HEADS = 4
GDN_K_DIM = 128
GDN_V_DIM = 128
GDN_SHORT_CONV = 5
GDN_W = GDN_HEADS * GDN_K_DIM
GDN_CHUNK = 256
GATE_LANE0 = QK_ROPE_DIM

def _params(semantics, vmem_limit=VMEM_LIMIT):
    return pltpu.CompilerParams(dimension_semantics=semantics, vmem_limit_bytes=vmem_limit)


def _resident(shape):
    return pl.BlockSpec(shape, lambda *_: (0,) * len(shape), pipeline_mode=pl.Buffered(1))


def _resident_layer(shape, layer):
    return pl.BlockSpec((None,) + tuple(shape), lambda *_: (layer,) + (0,) * len(shape),
                        pipeline_mode=pl.Buffered(1))


def _rms(x, w):
    return x * lax.rsqrt(jnp.mean(x * x, axis=-1, keepdims=True) + EPS) * w


def _sigmoid(x):
    return 1.0 / (1.0 + jnp.exp(-x))


def _silu(x):
    return x * _sigmoid(x)


def _rope_table_kernel(pos_ref, inv_ref, cos_ref, sin_ref):
    ang = pos_ref[...].astype(F32) * inv_ref[...]
    lane = lax.broadcasted_iota(jnp.int32, ang.shape, 1)
    valid = lane < QK_ROPE_DIM
    cos_ref[...] = jnp.where(valid, jnp.cos(ang), 0.0)
    sin_ref[...] = jnp.where(valid, jnp.sin(ang), 0.0)


def _rope_tables(positions):
    t = positions.size
    tm = min(t, 2048)
    half = QK_ROPE_DIM // 2
    inv = 1.0 / (ROPE_THETA ** (jnp.arange(0, QK_ROPE_DIM, 2, dtype=F32) / QK_ROPE_DIM))
    inv_row = jnp.zeros((1, LANES), F32).at[0, :half].set(inv).at[0, half:2 * half].set(inv)
    spec = pl.BlockSpec((tm, LANES), lambda i: (i, 0))
    return pl.pallas_call(
        _rope_table_kernel,
        grid=(t // tm,),
        in_specs=[pl.BlockSpec((tm, 1), lambda i: (i, 0)), _resident((1, LANES))],
        out_specs=[spec, spec],
        out_shape=[jax.ShapeDtypeStruct((t, LANES), F32)] * 2,
        compiler_params=_params(("parallel",)),
        name="rope_tables",
    )(positions.reshape(t, 1), inv_row)


def _in_proj_kernel(x_ref, g_ref, w_ref, *out_refs):
    half = x_ref.shape[0] // 2
    for r in (0, half):
        rows = slice(r, r + half)
        xn = _rms(x_ref[rows, :], g_ref[...]).astype(BF16)
        off = 0
        for o_ref in out_refs:
            n = o_ref.shape[1]
            o_ref[rows, :] = jnp.dot(xn, w_ref[:, off:off + n],
                                     preferred_element_type=F32).astype(o_ref.dtype)
            off += n


def _in_proj(x, gain, w_stack, layer, widths, tm=512):
    t, d = x.shape
    n = w_stack.shape[2]
    assert sum(widths) == n
    return pl.pallas_call(
        _in_proj_kernel,
        grid=(t // tm,),
        in_specs=[pl.BlockSpec((tm, d), lambda i: (i, 0)), _resident((1, d)),
                  _resident_layer((d, n), layer)],
        out_specs=[pl.BlockSpec((tm, wd), lambda i: (i, 0)) for wd in widths],
        out_shape=[jax.ShapeDtypeStruct((t, wd), F32) for wd in widths],
        compiler_params=_params(("parallel",)),
        name="in_proj",
    )(x, gain.reshape(1, d), w_stack)


def _mla_prep_kernel(cq_ref, ckv_ref, misc_ref, krot_ref, cos_ref, sin_ref, qn_ref, kvn_ref,
                     wq_ref, wkv_ref, q_out, kn_out, kr_out, v_out, *, q_scale):
    cos = cos_ref[...]
    sin = sin_ref[...]
    hw = MLA_HEADS * LANES
    cqn = _rms(cq_ref[...], qn_ref[...]).astype(BF16)
    q_nope = jnp.dot(cqn, wq_ref[:, :hw], preferred_element_type=F32)
    q_rope = jnp.dot(cqn, wq_ref[:, hw:2 * hw], preferred_element_type=F32)
    q_rot = jnp.dot(cqn, wq_ref[:, 2 * hw:], preferred_element_type=F32)
    ckvn = _rms(ckv_ref[...], kvn_ref[...]).astype(BF16)
    kn_out[...] = jnp.dot(ckvn, wkv_ref[:, :hw], preferred_element_type=F32).astype(kn_out.dtype)
    v_out[...] = jnp.dot(ckvn, wkv_ref[:, hw:], preferred_element_type=F32).astype(v_out.dtype)
    kr_out[...] = (misc_ref[...] * cos + krot_ref[...] * sin).astype(kr_out.dtype)
    for h in range(MLA_HEADS):
        blk = slice(h * LANES, (h + 1) * LANES)
        lo = slice(h * QK_PAD_DIM, h * QK_PAD_DIM + LANES)
        hi = slice(h * QK_PAD_DIM + LANES, (h + 1) * QK_PAD_DIM)
        q_out[:, lo] = (q_nope[:, blk] * q_scale).astype(q_out.dtype)
        q_out[:, hi] = ((q_rope[:, blk] * cos + q_rot[:, blk] * sin) * q_scale).astype(q_out.dtype)


def _mla_prep(cq, ckv, misc, krot, cos, sin, q_a_norm, kv_a_norm, wq_stack, wkv_stack, layer, q_scale,
              tm=512):
    t = cq.shape[0]
    row = lambda n: pl.BlockSpec((tm, n), lambda i: (i, 0))
    q_w = MLA_HEADS * QK_PAD_DIM
    kn_w = MLA_HEADS * QK_NOPE_DIM
    v_w = MLA_HEADS * V_HEAD_DIM
    bf = lambda n: jax.ShapeDtypeStruct((t, n), BF16)
    return pl.pallas_call(
        functools.partial(_mla_prep_kernel, q_scale=q_scale),
        grid=(t // tm,),
        in_specs=[row(Q_LORA_RANK), row(KV_LORA_RANK), row(LANES), row(LANES), row(LANES), row(LANES),
                  _resident((1, Q_LORA_RANK)), _resident((1, KV_LORA_RANK)),
                  _resident_layer(wq_stack.shape[1:], layer), _resident_layer(wkv_stack.shape[1:], layer)],
        out_specs=[row(q_w), row(kn_w), row(LANES), row(v_w)],
        out_shape=[bf(q_w), bf(kn_w), bf(LANES), bf(v_w)],
        compiler_params=_params(("parallel",)),
        name="mla_prep",
    )(cq, ckv, misc, krot, cos, sin, q_a_norm.reshape(1, -1), kv_a_norm.reshape(1, -1),
      wq_stack, wkv_stack)


ATTN_TK = 512


def _attn_kernel(q_ref, kn_ref, kr_ref, v_ref, o_ref, *, heads):
    tq = q_ref.shape[0]
    m = [jnp.full((tq, 1), -jnp.inf, F32)] * heads
    acc = [jnp.zeros((tq, V_PAD_DIM), F32)] * heads
    ones_col = jnp.where(lax.broadcasted_iota(jnp.int32, (ATTN_TK, LANES), 1) == 0, 1.0, 0.0).astype(BF16)
    for j in range(kn_ref.shape[0] // ATTN_TK):
        rows = slice(j * ATTN_TK, (j + 1) * ATTN_TK)
        k_rope = kr_ref[rows, :]
        for h in range(heads):
            head = slice(h * LANES, (h + 1) * LANES)
            k_op = jnp.concatenate([kn_ref[rows, head], k_rope], axis=1)
            v_op = jnp.concatenate([v_ref[rows, head], ones_col], axis=1)
            s = lax.dot_general(q_ref[:, h * QK_PAD_DIM:(h + 1) * QK_PAD_DIM], k_op,
                                (((1,), (1,)), ((), ())), preferred_element_type=F32)
            m_new = jnp.maximum(m[h], jnp.max(s, axis=-1, keepdims=True))
            p = jnp.exp2(s - m_new).astype(BF16)
            acc[h] = acc[h] * jnp.exp2(m[h] - m_new) + jnp.dot(p, v_op, preferred_element_type=F32)
            m[h] = m_new
    for h in range(heads):
        o_ref[:, h * V_HEAD_DIM:(h + 1) * V_HEAD_DIM] = (
            acc[h][:, :V_HEAD_DIM] / acc[h][:, V_HEAD_DIM:V_HEAD_DIM + 1]).astype(o_ref.dtype)


def _attention(q, k_nope, k_rope, v, batch, seq, tq=1024, heads=4):
    t = q.shape[0]
    nq = seq // tq
    return pl.pallas_call(
        functools.partial(_attn_kernel, heads=heads),
        grid=(batch, MLA_HEADS // heads, nq),
        in_specs=[pl.BlockSpec((tq, heads * QK_PAD_DIM), lambda b, h, i: (b * nq + i, h)),
                  pl.BlockSpec((seq, heads * QK_NOPE_DIM), lambda b, h, i: (b, h)),
                  pl.BlockSpec((seq, LANES), lambda b, h, i: (b, 0)),
                  pl.BlockSpec((seq, heads * V_HEAD_DIM), lambda b, h, i: (b, h))],
        out_specs=pl.BlockSpec((tq, heads * V_HEAD_DIM), lambda b, h, i: (b * nq + i, h)),
        out_shape=jax.ShapeDtypeStruct((t, MLA_HEADS * V_HEAD_DIM), BF16),
        compiler_params=_params(("parallel", "parallel", "arbitrary")),
        name="mla_attention",
    )(q, k_nope, k_rope, v)


def _halo_specs(tm, halo, width, n_rows):
    per = tm // halo
    last = n_rows // halo - 1
    return [pl.BlockSpec((tm, width), lambda i: (i, 0)),
            pl.BlockSpec((halo, width), lambda i: (jnp.maximum(i * per - 1, 0), 0)),
            pl.BlockSpec((halo, width), lambda i: (jnp.minimum((i + 1) * per, last), 0))]


def _seq_edges(tiles_per_seq):
    i = pl.program_id(0) % tiles_per_seq
    return i == 0, i == tiles_per_seq - 1


CONF_HALO = 16
CONF_ROWS = 32


def _conformer_kernel(cur_ref, prev_ref, next_ref, w_ref, b_ref, g_ref, beta_ref, o_ref, ext_ref,
                      shift_ref, *, tiles_per_seq):
    tm = cur_ref.shape[0]
    c = CONV_CHANNELS
    first, last = _seq_edges(tiles_per_seq)

    def glu(u):
        return u[:, :c] * _sigmoid(u[:, c:])

    ext_ref[0:CONF_HALO, :] = jnp.where(first, 0.0, glu(prev_ref[...]))
    ext_ref[CONF_HALO:CONF_HALO + tm, :] = glu(cur_ref[...])
    ext_ref[CONF_HALO + tm:, :] = jnp.where(last, 0.0, glu(next_ref[...]))
    n_shift = shift_ref.shape[1]
    for s in range(1, SUBLANES):
        shift_ref[s, :, :] = ext_ref[s:s + n_shift, :]
    pad = CONV_WIDTH // 2
    bias = b_ref[...]
    for r in range(0, tm, CONF_ROWS):
        acc = jnp.broadcast_to(bias, (CONF_ROWS, c))
        for d in range(CONV_WIDTH):
            start = CONF_HALO + r - pad + d
            s = start % SUBLANES
            base = start - s
            win = ext_ref[base:base + CONF_ROWS, :] if s == 0 else shift_ref[s, base:base + CONF_ROWS, :]
            acc = acc + win * w_ref[d:d + 1, :]
        mu = jnp.mean(acc, axis=-1, keepdims=True)
        cen = acc - mu
        var = jnp.mean(cen * cen, axis=-1, keepdims=True)
        y = cen * lax.rsqrt(var + EPS) * g_ref[...] + beta_ref[...]
        o_ref[r:r + CONF_ROWS, :] = _silu(y).astype(o_ref.dtype)


def _conformer(conv_in, dw_w, dw_b, ln_g, ln_b, seq, tm=512):
    t = conv_in.shape[0]
    c = CONV_CHANNELS
    return pl.pallas_call(
        functools.partial(_conformer_kernel, tiles_per_seq=seq // tm),
        grid=(t // tm,),
        in_specs=_halo_specs(tm, CONF_HALO, 2 * c, t) + [
            _resident((CONV_WIDTH, c)), _resident((1, c)), _resident((1, c)), _resident((1, c))],
        out_specs=pl.BlockSpec((tm, c), lambda i: (i, 0)),
        out_shape=jax.ShapeDtypeStruct((t, c), BF16),
        scratch_shapes=[pltpu.VMEM((tm + 2 * CONF_HALO, c), F32),
                        pltpu.VMEM((SUBLANES, tm + 2 * CONF_HALO - SUBLANES, c), F32)],
        compiler_params=_params(("parallel",)),
        name="conformer_conv",
    )(conv_in, conv_in, conv_in, dw_w.reshape(CONV_WIDTH, c), dw_b.reshape(1, c),
      ln_g.reshape(1, c), ln_b.reshape(1, c))


GDN_HALO = 8
GDN_ROWS = 64


def _gdn_prep_kernel(cur_ref, prev_ref, next_ref, misc_ref, w_ref, alog_ref, dtb_ref,
                     q_out, k_out, v_out, gate_out, ext_ref, *, tiles_per_seq):
    tm = cur_ref.shape[0]
    first, last = _seq_edges(tiles_per_seq)
    ext_ref[0:GDN_HALO, :] = jnp.where(first, 0.0, prev_ref[...])
    ext_ref[GDN_HALO:GDN_HALO + tm, :] = cur_ref[...]
    ext_ref[GDN_HALO + tm:, :] = jnp.where(last, 0.0, next_ref[...])
    pad = GDN_SHORT_CONV // 2
    for r in range(0, tm, GDN_ROWS):
        rows = slice(r, r + GDN_ROWS)
        for part, out in enumerate((q_out, k_out, v_out)):
            cols = slice(part * GDN_W, (part + 1) * GDN_W)
            acc = jnp.zeros((GDN_ROWS, GDN_W), F32)
            for d in range(GDN_SHORT_CONV):
                start = GDN_HALO + r - pad + d
                acc = acc + ext_ref[start:start + GDN_ROWS, cols] * w_ref[d:d + 1, cols]
            y = _silu(acc)
            if part == 2:
                out[rows, :] = y
                continue
            scale = GDN_K_DIM ** -0.5 if part == 0 else 1.0
            for h in range(GDN_HEADS):
                blk = slice(h * GDN_K_DIM, (h + 1) * GDN_K_DIM)
                yh = y[:, blk]
                inv = lax.rsqrt(jnp.sum(yh * yh, axis=-1, keepdims=True) + EPS)
                out[rows, blk] = yh * (inv * scale)
    m = misc_ref[...]
    lane = lax.broadcasted_iota(jnp.int32, m.shape, 1) - GATE_LANE0
    is_gate = (lane >= 0) & (lane < 4 * GDN_HEADS)
    is_a = is_gate & ((lane // GDN_HEADS) % 2 == 0)
    z = m + dtb_ref[...]
    softplus = jnp.maximum(z, 0.0) + jnp.log(1.0 + jnp.exp(-jnp.abs(z)))
    g = -jnp.exp(alog_ref[...]) * softplus
    gate_out[...] = jnp.where(is_a, g, jnp.where(is_gate, _sigmoid(m), 0.0))


def _gdn_prep(gqkv, misc, conv_w, a_log, dt_bias, seq, tm=512):
    t = gqkv.shape[0]
    w3 = 3 * GDN_W
    zeros = jnp.zeros((GDN_HEADS,), F32)
    lane_vals = lambda p: jnp.zeros((1, LANES), F32).at[0, GATE_LANE0:GATE_LANE0 + 4 * GDN_HEADS].set(
        jnp.concatenate([p[0], zeros, p[1], zeros]))
    row = lambda n: pl.BlockSpec((tm, n), lambda i: (i, 0))
    return pl.pallas_call(
        functools.partial(_gdn_prep_kernel, tiles_per_seq=seq // tm),
        grid=(t // tm,),
        in_specs=_halo_specs(tm, GDN_HALO, w3, t) + [
            row(LANES), _resident((GDN_SHORT_CONV, w3)), _resident((1, LANES)), _resident((1, LANES))],
        out_specs=[row(GDN_W), row(GDN_W), row(GDN_W), row(LANES)],
        out_shape=[jax.ShapeDtypeStruct((t, GDN_W), F32)] * 3 + [jax.ShapeDtypeStruct((t, LANES), F32)],
        scratch_shapes=[pltpu.VMEM((tm + 2 * GDN_HALO, w3), F32)],
        compiler_params=_params(("parallel",)),
        name="gdn_prep",
    )(gqkv, gqkv, gqkv, misc, conv_w.reshape(GDN_SHORT_CONV, w3), lane_vals(a_log), lane_vals(dt_bias))


def _bdot(a, b):
    return jnp.dot(a.astype(BF16), b.astype(BF16), preferred_element_type=F32)


def _bdot_nt(a, b):
    return lax.dot_general(a.astype(BF16), b.astype(BF16), (((1,), (1,)), ((), ())),
                           preferred_element_type=F32)


def _gdn_kernel(qf_ref, kf_ref, vf_ref, gf_ref, qb_ref, kb_ref, vb_ref, gb_ref, of_ref, ob_ref,
                state_ref, *, heads):
    c = qf_ref.shape[0]
    d = GDN_K_DIM

    @pl.when(pl.program_id(2) == 0)
    def _():
        state_ref[...] = jnp.zeros_like(state_ref)

    ri = lax.broadcasted_iota(jnp.int32, (c, c), 0)
    ci = lax.broadcasted_iota(jnp.int32, (c, c), 1)
    xor = ri ^ ci
    lane_row = lax.broadcasted_iota(jnp.int32, (1, LANES), 1)
    lane_col = lax.broadcasted_iota(jnp.int32, (LANES, 1), 0)

    chains = []
    for direction, (q_ref, k_ref, v_ref, g_ref, o_ref) in enumerate(
            ((qf_ref, kf_ref, vf_ref, gf_ref, of_ref), (qb_ref, kb_ref, vb_ref, gb_ref, ob_ref))):
        lower = direction == 0
        incl = (ri >= ci) if lower else (ri <= ci)
        strict = (ri > ci) if lower else (ri < ci)
        gates = g_ref[...]
        gc_all = jnp.dot(jnp.where(incl, 1.0, 0.0), gates, preferred_element_type=F32,
                         precision=lax.Precision.HIGHEST)
        gc_all_t = gc_all.T
        last = c - 1 if lower else 0
        for h in range(heads):
            blk = slice(h * d, (h + 1) * d)
            g_lane = GATE_LANE0 + 2 * GDN_HEADS * direction + pl.program_id(1) * heads + h
            gc = jnp.sum(jnp.where(lane_row == g_lane, gc_all, 0.0), axis=1, keepdims=True)
            beta = jnp.sum(jnp.where(lane_row == g_lane + GDN_HEADS, gates, 0.0), axis=1, keepdims=True)
            gc_row = jnp.sum(jnp.where(lane_col == g_lane, gc_all_t, 0.0), axis=0, keepdims=True)
            chains.append(dict(q=q_ref[:, blk], k=k_ref[:, blk], v=v_ref[:, blk], gc=gc, beta=beta,
                               gc_row=gc_row, g_tot=gc[last:last + 1, :], incl=incl, strict=strict,
                               o_ref=o_ref, blk=blk, idx=(direction, h)))

    for ch in chains:
        ch["kk"] = _bdot_nt(ch["k"], ch["k"])
    for ch in chains:
        ch["qk"] = _bdot_nt(ch["q"], ch["k"])
    for ch in chains:
        incl = ch["incl"]
        decay = jnp.where(incl, jnp.exp(jnp.where(incl, ch["gc"] - ch["gc_row"], 0.0)), 0.0)
        ch["a"] = jnp.where(ch["strict"], ch["kk"] * ch["beta"] * decay, 0.0)
        ch["qkd"] = (ch["qk"] * decay).astype(BF16)
        ch["t"] = jnp.where(ri == ci, 1.0, 0.0) - jnp.where(xor < 2, ch["a"], 0.0)
    b = 2
    while b < c:
        level = (xor >= b) & (xor < 2 * b)
        for ch in chains:
            ch["x"] = _bdot(jnp.where(level, ch["a"], 0.0), ch["t"])
        for ch in chains:
            ch["t"] = ch["t"] - _bdot(ch["t"], ch["x"])
        b *= 2
    for ch in chains:
        rhs = jnp.concatenate([ch["v"] * ch["beta"], ch["k"] * (ch["beta"] * jnp.exp(ch["gc"]))], axis=1)
        ch["uw"] = _bdot(ch["t"], rhs).astype(BF16)
    for ch in chains:
        ch["z"] = _bdot(ch["qkd"], ch["uw"])
    for ch in chains:
        k_til = ch["k"] * jnp.exp(ch["g_tot"] - ch["gc"])
        ch["y"] = _bdot(k_til.T, ch["uw"])
    for ch in chains:
        state = state_ref[ch["idx"]]
        q_eff = ch["q"] * jnp.exp(ch["gc"]) - ch["z"][:, d:]
        ch["o_ref"][:, ch["blk"]] = _bdot(q_eff, state) + ch["z"][:, :d]
        state_ref[ch["idx"]] = (state * jnp.exp(ch["g_tot"]) - _bdot(ch["y"][:, d:], state)
                                + ch["y"][:, :d])


def _gdn(q, k, v, gates, batch, seq, heads=GDN_HEADS):
    t = q.shape[0]
    c = GDN_CHUNK
    nc = seq // c
    hw = heads * GDN_K_DIM
    gate_f = pl.BlockSpec((c, LANES), lambda b, hp, i: (b * nc + i, 0))
    gate_b = pl.BlockSpec((c, LANES), lambda b, hp, i: (b * nc + nc - 1 - i, 0))
    head_f = pl.BlockSpec((c, hw), lambda b, hp, i: (b * nc + i, hp))
    head_b = pl.BlockSpec((c, hw), lambda b, hp, i: (b * nc + nc - 1 - i, hp))
    return pl.pallas_call(
        functools.partial(_gdn_kernel, heads=heads),
        grid=(batch, GDN_HEADS // heads, nc),
        in_specs=[head_f, head_f, head_f, gate_f, head_b, head_b, head_b, gate_b],
        out_specs=[head_f, head_b],
        out_shape=[jax.ShapeDtypeStruct((t, GDN_W), F32)] * 2,
        scratch_shapes=[pltpu.VMEM((2, heads, GDN_K_DIM, GDN_V_DIM), F32)],
        compiler_params=_params(("parallel", "parallel", "arbitrary")),
        name="gdn_delta_rule",
    )(q, k, v, gates, q, k, v, gates)


def _out_proj_kernel(x_ref, oa_ref, ob_ref, of_ref, obw_ref, z_ref, onorm_ref, w_ref, g_ref, o_ref):
    na = oa_ref.shape[1]
    nb = ob_ref.shape[1]
    half = x_ref.shape[0] // 2
    for r in (0, half):
        rows = slice(r, r + half)
        oc = of_ref[rows, :] + obw_ref[rows, :]
        z = z_ref[rows, :]
        parts = []
        for h in range(GDN_HEADS):
            blk = slice(h * GDN_V_DIM, (h + 1) * GDN_V_DIM)
            parts.append(_rms(oc[:, blk], onorm_ref[...]) * _silu(z[:, blk]))
        og = jnp.concatenate(parts, axis=1).astype(BF16)
        mix = (jnp.dot(oa_ref[rows, :], w_ref[:na, :], preferred_element_type=F32)
               + jnp.dot(ob_ref[rows, :], w_ref[na:na + nb, :], preferred_element_type=F32)
               + jnp.dot(og, w_ref[na + nb:, :], preferred_element_type=F32))
        o_ref[rows, :] = x_ref[rows, :] + _rms(mix, g_ref[...])


def _out_proj(x, o_a, o_b, o_f, o_bw, z, out_norm, w_out, gain, tm=512):
    t, d = x.shape
    row = lambda n: pl.BlockSpec((tm, n), lambda i: (i, 0))
    return pl.pallas_call(
        _out_proj_kernel,
        grid=(t // tm,),
        in_specs=[row(d), row(o_a.shape[1]), row(o_b.shape[1]), row(GDN_W), row(GDN_W), row(GDN_W),
                  _resident((1, GDN_V_DIM)), _resident(w_out.shape), _resident((1, d))],
        out_specs=row(d),
        out_shape=jax.ShapeDtypeStruct((t, d), F32),
        compiler_params=_params(("parallel",)),
        name="out_proj",
    )(x, o_a, o_b, o_f, o_bw, z, out_norm.reshape(1, -1), w_out, gain.reshape(1, d))


def _ffn_kernel(x_ref, gin_ref, wg_ref, wu_ref, wd_ref, gout_ref, o_ref, hn_ref):
    j = pl.program_id(1)

    @pl.when(j == 0)
    def _():
        hn_ref[...] = _rms(x_ref[...], gin_ref[...]).astype(BF16)
        o_ref[...] = jnp.zeros_like(o_ref)

    hn = hn_ref[...]
    gate = jnp.dot(hn, wg_ref[...], preferred_element_type=F32)
    up = jnp.dot(hn, wu_ref[...], preferred_element_type=F32)
    o_ref[...] += jnp.dot((_silu(gate) * up).astype(BF16), wd_ref[...], preferred_element_type=F32)

    @pl.when(j == pl.num_programs(1) - 1)
    def _():
        o_ref[...] = x_ref[...] + _rms(o_ref[...], gout_ref[...])


def _ffn(x, gain_in, w_gate, w_up, w_down, gain_out, tm=1024, tf=512):
    t, d = x.shape
    f = w_gate.shape[1]
    tm = min(tm, t)
    return pl.pallas_call(
        _ffn_kernel,
        grid=(t // tm, f // tf),
        in_specs=[pl.BlockSpec((tm, d), lambda i, j: (i, 0)),
                  _resident((1, d)),
                  pl.BlockSpec((d, tf), lambda i, j: (0, j)),
                  pl.BlockSpec((d, tf), lambda i, j: (0, j)),
                  pl.BlockSpec((tf, d), lambda i, j: (j, 0)),
                  _resident((1, d))],
        out_specs=pl.BlockSpec((tm, d), lambda i, j: (i, 0)),
        out_shape=jax.ShapeDtypeStruct((t, d), F32),
        scratch_shapes=[pltpu.VMEM((tm, d), BF16)],
        compiler_params=_params(("parallel", "arbitrary"), FFN_VMEM_LIMIT),
        name="swiglu_ffn",
    )(x, gain_in.reshape(1, d), w_gate, w_up, w_down, gain_out.reshape(1, d))


def _cast_kernel(x_ref, o_ref):
    o_ref[...] = x_ref[...].astype(o_ref.dtype)


def _to_bf16(w_stack, layer, rows=256):
    _, r, c = w_stack.shape
    return pl.pallas_call(
        _cast_kernel, grid=(r // rows,),
        in_specs=[pl.BlockSpec((None, rows, c), lambda i: (layer, i, 0))],
        out_specs=pl.BlockSpec((rows, c), lambda i: (i, 0)),
        out_shape=jax.ShapeDtypeStruct((r, c), BF16),
        compiler_params=_params(("parallel",)), name="to_bf16",
    )(w_stack)


def _rot_half_cols(w):
    half = w.shape[-1] // 2
    return jnp.concatenate([-w[..., half:], w[..., :half]], axis=-1)


def _pad_cols(w, n):
    return jnp.pad(w, [(0, 0)] * (w.ndim - 1) + [(0, n - w.shape[-1])])


IN_WIDTHS = (Q_LORA_RANK, KV_LORA_RANK, 2 * CONV_CHANNELS, 3 * GDN_W, GDN_W, LANES, LANES)


def _layout_w_in(w_in):
    sizes = (Q_LORA_RANK, KV_LORA_RANK, QK_ROPE_DIM, 2 * CONV_CHANNELS, 3 * GDN_W, GDN_W, 4 * GDN_HEADS)
    w_in = w_in.astype(BF16)
    c_q, c_kv, k_rope, conv, gqkv, gz, gates = jnp.split(w_in, np.cumsum(sizes)[:-1].tolist(), axis=-1)
    misc = _pad_cols(jnp.concatenate([k_rope, gates], axis=-1), LANES)
    krot = _pad_cols(_rot_half_cols(k_rope), LANES)
    return jnp.concatenate([c_q, c_kv, conv, gqkv, gz, misc, krot], axis=-1)


def _layout_w_uq(w_uq):
    lead = w_uq.shape[:-1]
    w = w_uq.reshape(*lead, MLA_HEADS, QK_HEAD_DIM)
    nope = w[..., :QK_NOPE_DIM]
    rope = w[..., QK_NOPE_DIM:]
    parts = [nope, _pad_cols(rope, LANES), _pad_cols(_rot_half_cols(rope), LANES)]
    return jnp.concatenate([p.reshape(*lead, MLA_HEADS * LANES) for p in parts], axis=-1).astype(BF16)


def _layout_w_ukv(w_ukv):
    lead = w_ukv.shape[:-1]
    w = w_ukv.reshape(*lead, MLA_HEADS, QK_NOPE_DIM + V_HEAD_DIM)
    return jnp.concatenate([w[..., :QK_NOPE_DIM].reshape(*lead, -1), w[..., QK_NOPE_DIM:].reshape(*lead, -1)],
                           axis=-1).astype(BF16)


def _layer(x, cos, sin, batch, seq, layer, w_in_stack, w_uq_stack, w_ukv_stack, pre_mix_norm, q_a_norm,
           kv_a_norm, conv_dw_w, conv_dw_b, conv_ln_g, conv_ln_b, gdn_conv_w, gdn_a_log, gdn_dt_bias,
           gdn_out_norm, w_out, post_mix_norm, pre_ffn_norm, w_gate, w_up, w_down, post_ffn_norm):
    cq, ckv, conv_in, gqkv, gz, misc, krot = _in_proj(x, pre_mix_norm, w_in_stack, layer, IN_WIDTHS)
    q_scale = QK_HEAD_DIM ** -0.5 * float(np.log2(np.e))
    q, k_nope, k_rope, v = _mla_prep(cq, ckv, misc, krot, cos, sin, q_a_norm, kv_a_norm,
                                     w_uq_stack, w_ukv_stack, layer, q_scale)
    o_a = _attention(q, k_nope, k_rope, v, batch, seq)
    o_b = _conformer(conv_in, conv_dw_w, conv_dw_b, conv_ln_g, conv_ln_b, seq)
    gq, gk, gv, gates = _gdn_prep(gqkv, misc, gdn_conv_w, gdn_a_log, gdn_dt_bias, seq)
    o_f, o_bw = _gdn(gq, gk, gv, gates, batch, seq)
    x = _out_proj(x, o_a, o_b, o_f, o_bw, gz, gdn_out_norm, w_out, post_mix_norm)
    return _ffn(x, pre_ffn_norm, w_gate, w_up, w_down, post_ffn_norm)


def kernel(x, positions, pre_mix_norm, w_in, q_a_norm, w_uq, kv_a_norm, w_ukv, conv_dw_w, conv_dw_b,
           conv_ln_g, conv_ln_b, gdn_conv_w, gdn_a_log, gdn_dt_bias, gdn_out_norm, w_out, post_mix_norm,
           pre_ffn_norm, w_gate, w_up, w_down, post_ffn_norm):
    batch, seq, d = x.shape
    cos, sin = _rope_tables(positions)
    h = x.reshape(batch * seq, d)
    stacks = (_layout_w_in(w_in), _layout_w_uq(w_uq), _layout_w_ukv(w_ukv))
    per_layer = (pre_mix_norm, q_a_norm, kv_a_norm, conv_dw_w, conv_dw_b, conv_ln_g, conv_ln_b,
                 gdn_conv_w, gdn_a_log, gdn_dt_bias, gdn_out_norm, w_out, post_mix_norm, pre_ffn_norm,
                 w_gate, w_up, w_down, post_ffn_norm)
    big = (w_out, w_gate, w_up, w_down)
    for l in range(pre_mix_norm.shape[0]):
        h = _layer(h, cos, sin, batch, seq, l, *stacks,
                   *(_to_bf16(p, l) if any(p is b for b in big) else p[l] for p in per_layer))
    return h.reshape(batch, seq, d)
```

```python
import functools

import numpy as np
import jax
import jax.numpy as jnp
from jax import lax
from jax.experimental import pallas as pl
from jax.experimental.pallas import tpu as pltpu

F32 = jnp.float32
BF16 = jnp.bfloat16

EPS = 1e-6
LANES = 128
SUBLANES = 8
VMEM_LIMIT = 56 * 1024 * 1024
FFN_VMEM_LIMIT = 62 * 1024 * 1024

MLA_HEADS = 8
Q_LORA_RANK = 512
KV_LORA_RANK = 512
QK_NOPE_DIM = 128
QK_ROPE_DIM = 64
QK_HEAD_DIM = QK_NOPE_DIM + QK_ROPE_DIM
V_HEAD_DIM = 128
ROPE_THETA = 10000.0
QK_PAD_DIM = 2 * LANES
V_PAD_DIM = 2 * LANES
CONV_CHANNELS = 512
CONV_WIDTH = 31
GDN_HEADS = 4
GDN_K_DIM = 128
GDN_V_DIM = 128
GDN_SHORT_CONV = 5
GDN_W = GDN_HEADS * GDN_K_DIM
GDN_CHUNK = 256
GATE_LANE0 = QK_ROPE_DIM

def _params(semantics, vmem_limit=VMEM_LIMIT):
    return pltpu.CompilerParams(dimension_semantics=semantics, vmem_limit_bytes=vmem_limit)


def _resident(shape):
    return pl.BlockSpec(shape, lambda *_: (0,) * len(shape), pipeline_mode=pl.Buffered(1))


def _resident_layer(shape, layer):
    return pl.BlockSpec((None,) + tuple(shape), lambda *_: (layer,) + (0,) * len(shape),
                        pipeline_mode=pl.Buffered(1))


def _rms(x, w):
    return x * lax.rsqrt(jnp.mean(x * x, axis=-1, keepdims=True) + EPS) * w


def _sigmoid(x):
    return 1.0 / (1.0 + jnp.exp(-x))


def _silu(x):
    return x * _sigmoid(x)


def _rope_table_kernel(pos_ref, inv_ref, cos_ref, sin_ref):
    ang = pos_ref[...].astype(F32) * inv_ref[...]
    lane = lax.broadcasted_iota(jnp.int32, ang.shape, 1)
    valid = lane < QK_ROPE_DIM
    cos_ref[...] = jnp.where(valid, jnp.cos(ang), 0.0)
    sin_ref[...] = jnp.where(valid, jnp.sin(ang), 0.0)


def _rope_tables(positions):
    t = positions.size
    tm = min(t, 2048)
    half = QK_ROPE_DIM // 2
    inv = 1.0 / (ROPE_THETA ** (jnp.arange(0, QK_ROPE_DIM, 2, dtype=F32) / QK_ROPE_DIM))
    inv_row = jnp.zeros((1, LANES), F32).at[0, :half].set(inv).at[0, half:2 * half].set(inv)
    spec = pl.BlockSpec((tm, LANES), lambda i: (i, 0))
    return pl.pallas_call(
        _rope_table_kernel,
        grid=(t // tm,),
        in_specs=[pl.BlockSpec((tm, 1), lambda i: (i, 0)), _resident((1, LANES))],
        out_specs=[spec, spec],
        out_shape=[jax.ShapeDtypeStruct((t, LANES), F32)] * 2,
        compiler_params=_params(("parallel",)),
        name="rope_tables",
    )(positions.reshape(t, 1), inv_row)


def _in_proj_kernel(x_ref, g_ref, w_ref, *out_refs):
    half = x_ref.shape[0] // 2
    for r in (0, half):
        rows = slice(r, r + half)
        xn = _rms(x_ref[rows, :], g_ref[...]).astype(BF16)
        off = 0
        for o_ref in out_refs:
            n = o_ref.shape[1]
            o_ref[rows, :] = jnp.dot(xn, w_ref[:, off:off + n],
                                     preferred_element_type=F32).astype(o_ref.dtype)
            off += n


def _in_proj(x, gain, w_stack, layer, widths, tm=512):
    t, d = x.shape
    n = w_stack.shape[2]
    assert sum(widths) == n
    return pl.pallas_call(
        _in_proj_kernel,
        grid=(t // tm,),
        in_specs=[pl.BlockSpec((tm, d), lambda i: (i, 0)), _resident((1, d)),
                  _resident_layer((d, n), layer)],
        out_specs=[pl.BlockSpec((tm, wd), lambda i: (i, 0)) for wd in widths],
        out_shape=[jax.ShapeDtypeStruct((t, wd), F32) for wd in widths],
        compiler_params=_params(("parallel",)),
        name="in_proj",
    )(x, gain.reshape(1, d), w_stack)


def _mla_prep_kernel(cq_ref, ckv_ref, misc_ref, krot_ref, cos_ref, sin_ref, qn_ref, kvn_ref,
                     wq_ref, wkv_ref, q_out, kn_out, kr_out, v_out, *, q_scale):
    cos = cos_ref[...]
    sin = sin_ref[...]
    hw = MLA_HEADS * LANES
    cqn = _rms(cq_ref[...], qn_ref[...]).astype(BF16)
    q_nope = jnp.dot(cqn, wq_ref[:, :hw], preferred_element_type=F32)
    q_rope = jnp.dot(cqn, wq_ref[:, hw:2 * hw], preferred_element_type=F32)
    q_rot = jnp.dot(cqn, wq_ref[:, 2 * hw:], preferred_element_type=F32)
    ckvn = _rms(ckv_ref[...], kvn_ref[...]).astype(BF16)
    kn_out[...] = jnp.dot(ckvn, wkv_ref[:, :hw], preferred_element_type=F32).astype(kn_out.dtype)
    v_out[...] = jnp.dot(ckvn, wkv_ref[:, hw:], preferred_element_type=F32).astype(v_out.dtype)
    kr_out[...] = (misc_ref[...] * cos + krot_ref[...] * sin).astype(kr_out.dtype)
    for h in range(MLA_HEADS):
        blk = slice(h * LANES, (h + 1) * LANES)
        lo = slice(h * QK_PAD_DIM, h * QK_PAD_DIM + LANES)
        hi = slice(h * QK_PAD_DIM + LANES, (h + 1) * QK_PAD_DIM)
        q_out[:, lo] = (q_nope[:, blk] * q_scale).astype(q_out.dtype)
        q_out[:, hi] = ((q_rope[:, blk] * cos + q_rot[:, blk] * sin) * q_scale).astype(q_out.dtype)


def _mla_prep(cq, ckv, misc, krot, cos, sin, q_a_norm, kv_a_norm, wq_stack, wkv_stack, layer, q_scale,
              tm=512):
    t = cq.shape[0]
    row = lambda n: pl.BlockSpec((tm, n), lambda i: (i, 0))
    q_w = MLA_HEADS * QK_PAD_DIM
    kn_w = MLA_HEADS * QK_NOPE_DIM
    v_w = MLA_HEADS * V_HEAD_DIM
    bf = lambda n: jax.ShapeDtypeStruct((t, n), BF16)
    return pl.pallas_call(
        functools.partial(_mla_prep_kernel, q_scale=q_scale),
        grid=(t // tm,),
        in_specs=[row(Q_LORA_RANK), row(KV_LORA_RANK), row(LANES), row(LANES), row(LANES), row(LANES),
                  _resident((1, Q_LORA_RANK)), _resident((1, KV_LORA_RANK)),
                  _resident_layer(wq_stack.shape[1:], layer), _resident_layer(wkv_stack.shape[1:], layer)],
        out_specs=[row(q_w), row(kn_w), row(LANES), row(v_w)],
        out_shape=[bf(q_w), bf(kn_w), bf(LANES), bf(v_w)],
        compiler_params=_params(("parallel",)),
        name="mla_prep",
    )(cq, ckv, misc, krot, cos, sin, q_a_norm.reshape(1, -1), kv_a_norm.reshape(1, -1),
      wq_stack, wkv_stack)


ATTN_TK = 512


def _attn_kernel(q_ref, kn_ref, kr_ref, v_ref, o_ref, *, heads):
    tq = q_ref.shape[0]
    m = [jnp.full((tq, 1), -jnp.inf, F32)] * heads
    acc = [jnp.zeros((tq, V_PAD_DIM), F32)] * heads
    ones_col = jnp.where(lax.broadcasted_iota(jnp.int32, (ATTN_TK, LANES), 1) == 0, 1.0, 0.0).astype(BF16)
    for j in range(kn_ref.shape[0] // ATTN_TK):
        rows = slice(j * ATTN_TK, (j + 1) * ATTN_TK)
        k_rope = kr_ref[rows, :]
        for h in range(heads):
            head = slice(h * LANES, (h + 1) * LANES)
            k_op = jnp.concatenate([kn_ref[rows, head], k_rope], axis=1)
            v_op = jnp.concatenate([v_ref[rows, head], ones_col], axis=1)
            s = lax.dot_general(q_ref[:, h * QK_PAD_DIM:(h + 1) * QK_PAD_DIM], k_op,
                                (((1,), (1,)), ((), ())), preferred_element_type=F32)
            m_new = jnp.maximum(m[h], jnp.max(s, axis=-1, keepdims=True))
            p = jnp.exp2(s - m_new).astype(BF16)
            acc[h] = acc[h] * jnp.exp2(m[h] - m_new) + jnp.dot(p, v_op, preferred_element_type=F32)
            m[h] = m_new
    for h in range(heads):
        o_ref[:, h * V_HEAD_DIM:(h + 1) * V_HEAD_DIM] = (
            acc[h][:, :V_HEAD_DIM] / acc[h][:, V_HEAD_DIM:V_HEAD_DIM + 1]).astype(o_ref.dtype)


def _attention(q, k_nope, k_rope, v, batch, seq, tq=1024, heads=4):
    t = q.shape[0]
    nq = seq // tq
    return pl.pallas_call(
        functools.partial(_attn_kernel, heads=heads),
        grid=(batch, MLA_HEADS // heads, nq),
        in_specs=[pl.BlockSpec((tq, heads * QK_PAD_DIM), lambda b, h, i: (b * nq + i, h)),
                  pl.BlockSpec((seq, heads * QK_NOPE_DIM), lambda b, h, i: (b, h)),
                  pl.BlockSpec((seq, LANES), lambda b, h, i: (b, 0)),
                  pl.BlockSpec((seq, heads * V_HEAD_DIM), lambda b, h, i: (b, h))],
        out_specs=pl.BlockSpec((tq, heads * V_HEAD_DIM), lambda b, h, i: (b * nq + i, h)),
        out_shape=jax.ShapeDtypeStruct((t, MLA_HEADS * V_HEAD_DIM), BF16),
        compiler_params=_params(("parallel", "parallel", "arbitrary")),
        name="mla_attention",
    )(q, k_nope, k_rope, v)


def _halo_specs(tm, halo, width, n_rows):
    per = tm // halo
    last = n_rows // halo - 1
    return [pl.BlockSpec((tm, width), lambda i: (i, 0)),
            pl.BlockSpec((halo, width), lambda i: (jnp.maximum(i * per - 1, 0), 0)),
            pl.BlockSpec((halo, width), lambda i: (jnp.minimum((i + 1) * per, last), 0))]


def _seq_edges(tiles_per_seq):
    i = pl.program_id(0) % tiles_per_seq
    return i == 0, i == tiles_per_seq - 1


CONF_HALO = 16
CONF_ROWS = 128


def _conformer_kernel(cur_ref, prev_ref, next_ref, w_ref, b_ref, g_ref, beta_ref, o_ref, ext_ref,
                      shift_ref, *, tiles_per_seq):
    tm = cur_ref.shape[0]
    c = CONV_CHANNELS
    first, last = _seq_edges(tiles_per_seq)

    def glu(u):
        return u[:, :c] * _sigmoid(u[:, c:])

    ext_ref[0:CONF_HALO, :] = jnp.where(first, 0.0, glu(prev_ref[...]))
    ext_ref[CONF_HALO:CONF_HALO + tm, :] = glu(cur_ref[...])
    ext_ref[CONF_HALO + tm:, :] = jnp.where(last, 0.0, glu(next_ref[...]))
    n_shift = shift_ref.shape[1]
    for s in range(1, SUBLANES):
        shift_ref[s, :, :] = ext_ref[s:s + n_shift, :]
    pad = CONV_WIDTH // 2
    bias = b_ref[...]
    for r in range(0, tm, CONF_ROWS):
        acc = jnp.broadcast_to(bias, (CONF_ROWS, c))
        for d in range(CONV_WIDTH):
            start = CONF_HALO + r - pad + d
            s = start % SUBLANES
            base = start - s
            win = ext_ref[base:base + CONF_ROWS, :] if s == 0 else shift_ref[s, base:base + CONF_ROWS, :]
            acc = acc + win * w_ref[d:d + 1, :]
        mu = jnp.mean(acc, axis=-1, keepdims=True)
        cen = acc - mu
        var = jnp.mean(cen * cen, axis=-1, keepdims=True)
        y = cen * lax.rsqrt(var + EPS) * g_ref[...] + beta_ref[...]
        o_ref[r:r + CONF_ROWS, :] = _silu(y).astype(o_ref.dtype)


def _conformer(conv_in, dw_w, dw_b, ln_g, ln_b, seq, tm=512):
    t = conv_in.shape[0]
    c = CONV_CHANNELS
    return pl.pallas_call(
        functools.partial(_conformer_kernel, tiles_per_seq=seq // tm),
        grid=(t // tm,),
        in_specs=_halo_specs(tm, CONF_HALO, 2 * c, t) + [
            _resident((CONV_WIDTH, c)), _resident((1, c)), _resident((1, c)), _resident((1, c))],
        out_specs=pl.BlockSpec((tm, c), lambda i: (i, 0)),
        out_shape=jax.ShapeDtypeStruct((t, c), BF16),
        scratch_shapes=[pltpu.VMEM((tm + 2 * CONF_HALO, c), F32),
                        pltpu.VMEM((SUBLANES, tm + 2 * CONF_HALO - SUBLANES, c), F32)],
        compiler_params=_params(("parallel",)),
        name="conformer_conv",
    )(conv_in, conv_in, conv_in, dw_w.reshape(CONV_WIDTH, c), dw_b.reshape(1, c),
      ln_g.reshape(1, c), ln_b.reshape(1, c))


GDN_HALO = 8
GDN_ROWS = 64


def _gdn_prep_kernel(cur_ref, prev_ref, next_ref, misc_ref, w_ref, alog_ref, dtb_ref,
                     q_out, k_out, v_out, gate_out, ext_ref, *, tiles_per_seq):
    tm = cur_ref.shape[0]
    first, last = _seq_edges(tiles_per_seq)
    ext_ref[0:GDN_HALO, :] = jnp.where(first, 0.0, prev_ref[...])
    ext_ref[GDN_HALO:GDN_HALO + tm, :] = cur_ref[...]
    ext_ref[GDN_HALO + tm:, :] = jnp.where(last, 0.0, next_ref[...])
    pad = GDN_SHORT_CONV // 2
    for r in range(0, tm, GDN_ROWS):
        rows = slice(r, r + GDN_ROWS)
        for part, out in enumerate((q_out, k_out, v_out)):
            cols = slice(part * GDN_W, (part + 1) * GDN_W)
            acc = jnp.zeros((GDN_ROWS, GDN_W), F32)
            for d in range(GDN_SHORT_CONV):
                start = GDN_HALO + r - pad + d
                acc = acc + ext_ref[start:start + GDN_ROWS, cols] * w_ref[d:d + 1, cols]
            y = _silu(acc)
            if part == 2:
                out[rows, :] = y
                continue
            scale = GDN_K_DIM ** -0.5 if part == 0 else 1.0
            for h in range(GDN_HEADS):
                blk = slice(h * GDN_K_DIM, (h + 1) * GDN_K_DIM)
                yh = y[:, blk]
                inv = lax.rsqrt(jnp.sum(yh * yh, axis=-1, keepdims=True) + EPS)
                out[rows, blk] = yh * (inv * scale)
    m = misc_ref[...]
    lane = lax.broadcasted_iota(jnp.int32, m.shape, 1) - GATE_LANE0
    is_gate = (lane >= 0) & (lane < 4 * GDN_HEADS)
    is_a = is_gate & ((lane // GDN_HEADS) % 2 == 0)
    z = m + dtb_ref[...]
    softplus = jnp.maximum(z, 0.0) + jnp.log(1.0 + jnp.exp(-jnp.abs(z)))
    g = -jnp.exp(alog_ref[...]) * softplus
    gate_out[...] = jnp.where(is_a, g, jnp.where(is_gate, _sigmoid(m), 0.0))


def _gdn_prep(gqkv, misc, conv_w, a_log, dt_bias, seq, tm=512):
    t = gqkv.shape[0]
    w3 = 3 * GDN_W
    zeros = jnp.zeros((GDN_HEADS,), F32)
    lane_vals = lambda p: jnp.zeros((1, LANES), F32).at[0, GATE_LANE0:GATE_LANE0 + 4 * GDN_HEADS].set(
        jnp.concatenate([p[0], zeros, p[1], zeros]))
    row = lambda n: pl.BlockSpec((tm, n), lambda i: (i, 0))
    return pl.pallas_call(
        functools.partial(_gdn_prep_kernel, tiles_per_seq=seq // tm),
        grid=(t // tm,),
        in_specs=_halo_specs(tm, GDN_HALO, w3, t) + [
            row(LANES), _resident((GDN_SHORT_CONV, w3)), _resident((1, LANES)), _resident((1, LANES))],
        out_specs=[row(GDN_W), row(GDN_W), row(GDN_W), row(LANES)],
        out_shape=[jax.ShapeDtypeStruct((t, GDN_W), F32)] * 3 + [jax.ShapeDtypeStruct((t, LANES), F32)],
        scratch_shapes=[pltpu.VMEM((tm + 2 * GDN_HALO, w3), F32)],
        compiler_params=_params(("parallel",)),
        name="gdn_prep",
    )(gqkv, gqkv, gqkv, misc, conv_w.reshape(GDN_SHORT_CONV, w3), lane_vals(a_log), lane_vals(dt_bias))


def _bdot(a, b):
    return jnp.dot(a.astype(BF16), b.astype(BF16), preferred_element_type=F32)


def _bdot_nt(a, b):
    return lax.dot_general(a.astype(BF16), b.astype(BF16), (((1,), (1,)), ((), ())),
                           preferred_element_type=F32)


def _gdn_kernel(qf_ref, kf_ref, vf_ref, gf_ref, qb_ref, kb_ref, vb_ref, gb_ref, of_ref, ob_ref,
                state_ref, *, heads):
    c = qf_ref.shape[0]
    d = GDN_K_DIM

    @pl.when(pl.program_id(2) == 0)
    def _():
        state_ref[...] = jnp.zeros_like(state_ref)

    ri = lax.broadcasted_iota(jnp.int32, (c, c), 0)
    ci = lax.broadcasted_iota(jnp.int32, (c, c), 1)
    xor = ri ^ ci
    lane_row = lax.broadcasted_iota(jnp.int32, (1, LANES), 1)
    lane_col = lax.broadcasted_iota(jnp.int32, (LANES, 1), 0)

    chains = []
    for direction, (q_ref, k_ref, v_ref, g_ref, o_ref) in enumerate(
            ((qf_ref, kf_ref, vf_ref, gf_ref, of_ref), (qb_ref, kb_ref, vb_ref, gb_ref, ob_ref))):
        lower = direction == 0
        incl = (ri >= ci) if lower else (ri <= ci)
        strict = (ri > ci) if lower else (ri < ci)
        gates = g_ref[...]
        gc_all = jnp.dot(jnp.where(incl, 1.0, 0.0), gates, preferred_element_type=F32,
                         precision=lax.Precision.HIGHEST)
        gc_all_t = gc_all.T
        last = c - 1 if lower else 0
        for h in range(heads):
            blk = slice(h * d, (h + 1) * d)
            g_lane = GATE_LANE0 + 2 * GDN_HEADS * direction + pl.program_id(1) * heads + h
            gc = jnp.sum(jnp.where(lane_row == g_lane, gc_all, 0.0), axis=1, keepdims=True)
            beta = jnp.sum(jnp.where(lane_row == g_lane + GDN_HEADS, gates, 0.0), axis=1, keepdims=True)
            gc_row = jnp.sum(jnp.where(lane_col == g_lane, gc_all_t, 0.0), axis=0, keepdims=True)
            chains.append(dict(q=q_ref[:, blk], k=k_ref[:, blk], v=v_ref[:, blk], gc=gc, beta=beta,
                               gc_row=gc_row, g_tot=gc[last:last + 1, :], incl=incl, strict=strict,
                               o_ref=o_ref, blk=blk, idx=(direction, h)))

    for ch in chains:
        ch["kk"] = _bdot_nt(ch["k"], ch["k"])
    for ch in chains:
        ch["qk"] = _bdot_nt(ch["q"], ch["k"])
    for ch in chains:
        incl = ch["incl"]
        decay = jnp.where(incl, jnp.exp(jnp.where(incl, ch["gc"] - ch["gc_row"], 0.0)), 0.0)
        ch["a"] = jnp.where(ch["strict"], ch["kk"] * ch["beta"] * decay, 0.0)
        ch["qkd"] = (ch["qk"] * decay).astype(BF16)
        ch["t"] = jnp.where(ri == ci, 1.0, 0.0) - jnp.where(xor < 2, ch["a"], 0.0)
    b = 2
    while b < c:
        level = (xor >= b) & (xor < 2 * b)
        for ch in chains:
            ch["x"] = _bdot(jnp.where(level, ch["a"], 0.0), ch["t"])
        for ch in chains:
            ch["t"] = ch["t"] - _bdot(ch["t"], ch["x"])
        b *= 2
    for ch in chains:
        rhs = jnp.concatenate([ch["v"] * ch["beta"], ch["k"] * (ch["beta"] * jnp.exp(ch["gc"]))], axis=1)
        ch["uw"] = _bdot(ch["t"], rhs).astype(BF16)
    for ch in chains:
        ch["z"] = _bdot(ch["qkd"], ch["uw"])
    for ch in chains:
        k_til = ch["k"] * jnp.exp(ch["g_tot"] - ch["gc"])
        ch["y"] = _bdot(k_til.T, ch["uw"])
    for ch in chains:
        state = state_ref[ch["idx"]]
        q_eff = ch["q"] * jnp.exp(ch["gc"]) - ch["z"][:, d:]
        ch["o_ref"][:, ch["blk"]] = _bdot(q_eff, state) + ch["z"][:, :d]
        state_ref[ch["idx"]] = (state * jnp.exp(ch["g_tot"]) - _bdot(ch["y"][:, d:], state)
                                + ch["y"][:, :d])


def _gdn(q, k, v, gates, batch, seq, heads=GDN_HEADS):
    t = q.shape[0]
    c = GDN_CHUNK
    nc = seq // c
    hw = heads * GDN_K_DIM
    gate_f = pl.BlockSpec((c, LANES), lambda b, hp, i: (b * nc + i, 0))
    gate_b = pl.BlockSpec((c, LANES), lambda b, hp, i: (b * nc + nc - 1 - i, 0))
    head_f = pl.BlockSpec((c, hw), lambda b, hp, i: (b * nc + i, hp))
    head_b = pl.BlockSpec((c, hw), lambda b, hp, i: (b * nc + nc - 1 - i, hp))
    return pl.pallas_call(
        functools.partial(_gdn_kernel, heads=heads),
        grid=(batch, GDN_HEADS // heads, nc),
        in_specs=[head_f, head_f, head_f, gate_f, head_b, head_b, head_b, gate_b],
        out_specs=[head_f, head_b],
        out_shape=[jax.ShapeDtypeStruct((t, GDN_W), F32)] * 2,
        scratch_shapes=[pltpu.VMEM((2, heads, GDN_K_DIM, GDN_V_DIM), F32)],
        compiler_params=_params(("parallel", "parallel", "arbitrary")),
        name="gdn_delta_rule",
    )(q, k, v, gates, q, k, v, gates)


def _out_proj_kernel(x_ref, oa_ref, ob_ref, of_ref, obw_ref, z_ref, onorm_ref, w_ref, g_ref, o_ref):
    na = oa_ref.shape[1]
    nb = ob_ref.shape[1]
    half = x_ref.shape[0] // 2
    for r in (0, half):
        rows = slice(r, r + half)
        oc = of_ref[rows, :] + obw_ref[rows, :]
        z = z_ref[rows, :]
        parts = []
        for h in range(GDN_HEADS):
            blk = slice(h * GDN_V_DIM, (h + 1) * GDN_V_DIM)
            parts.append(_rms(oc[:, blk], onorm_ref[...]) * _silu(z[:, blk]))
        og = jnp.concatenate(parts, axis=1).astype(BF16)
        mix = (jnp.dot(oa_ref[rows, :], w_ref[:na, :], preferred_element_type=F32)
               + jnp.dot(ob_ref[rows, :], w_ref[na:na + nb, :], preferred_element_type=F32)
               + jnp.dot(og, w_ref[na + nb:, :], preferred_element_type=F32))
        o_ref[rows, :] = x_ref[rows, :] + _rms(mix, g_ref[...])


def _out_proj(x, o_a, o_b, o_f, o_bw, z, out_norm, w_out, gain, tm=512):
    t, d = x.shape
    row = lambda n: pl.BlockSpec((tm, n), lambda i: (i, 0))
    return pl.pallas_call(
        _out_proj_kernel,
        grid=(t // tm,),
        in_specs=[row(d), row(o_a.shape[1]), row(o_b.shape[1]), row(GDN_W), row(GDN_W), row(GDN_W),
                  _resident((1, GDN_V_DIM)), _resident(w_out.shape), _resident((1, d))],
        out_specs=row(d),
        out_shape=jax.ShapeDtypeStruct((t, d), F32),
        compiler_params=_params(("parallel",)),
        name="out_proj",
    )(x, o_a, o_b, o_f, o_bw, z, out_norm.reshape(1, -1), w_out, gain.reshape(1, d))


def _ffn_kernel(x_ref, gin_ref, wg_ref, wu_ref, wd_ref, gout_ref, o_ref, hn_ref):
    j = pl.program_id(1)

    @pl.when(j == 0)
    def _():
        hn_ref[...] = _rms(x_ref[...], gin_ref[...]).astype(BF16)
        o_ref[...] = jnp.zeros_like(o_ref)

    hn = hn_ref[...]
    gate = jnp.dot(hn, wg_ref[...], preferred_element_type=F32)
    up = jnp.dot(hn, wu_ref[...], preferred_element_type=F32)
    o_ref[...] += jnp.dot((_silu(gate) * up).astype(BF16), wd_ref[...], preferred_element_type=F32)

    @pl.when(j == pl.num_programs(1) - 1)
    def _():
        o_ref[...] = x_ref[...] + _rms(o_ref[...], gout_ref[...])


def _ffn(x, gain_in, w_gate, w_up, w_down, gain_out, tm=1024, tf=512):
    t, d = x.shape
    f = w_gate.shape[1]
    tm = min(tm, t)
    return pl.pallas_call(
        _ffn_kernel,
        grid=(t // tm, f // tf),
        in_specs=[pl.BlockSpec((tm, d), lambda i, j: (i, 0)),
                  _resident((1, d)),
                  pl.BlockSpec((d, tf), lambda i, j: (0, j)),
                  pl.BlockSpec((d, tf), lambda i, j: (0, j)),
                  pl.BlockSpec((tf, d), lambda i, j: (j, 0)),
                  _resident((1, d))],
        out_specs=pl.BlockSpec((tm, d), lambda i, j: (i, 0)),
        out_shape=jax.ShapeDtypeStruct((t, d), F32),
        scratch_shapes=[pltpu.VMEM((tm, d), BF16)],
        compiler_params=_params(("parallel", "arbitrary"), FFN_VMEM_LIMIT),
        name="swiglu_ffn",
    )(x, gain_in.reshape(1, d), w_gate, w_up, w_down, gain_out.reshape(1, d))


def _cast_kernel(x_ref, o_ref):
    o_ref[...] = x_ref[...].astype(o_ref.dtype)


def _to_bf16(w_stack, layer, rows=256):
    _, r, c = w_stack.shape
    return pl.pallas_call(
        _cast_kernel, grid=(r // rows,),
        in_specs=[pl.BlockSpec((None, rows, c), lambda i: (layer, i, 0))],
        out_specs=pl.BlockSpec((rows, c), lambda i: (i, 0)),
        out_shape=jax.ShapeDtypeStruct((r, c), BF16),
        compiler_params=_params(("parallel",)), name="to_bf16",
    )(w_stack)


def _rot_half_cols(w):
    half = w.shape[-1] // 2
    return jnp.concatenate([-w[..., half:], w[..., :half]], axis=-1)


def _pad_cols(w, n):
    return jnp.pad(w, [(0, 0)] * (w.ndim - 1) + [(0, n - w.shape[-1])])


IN_WIDTHS = (Q_LORA_RANK, KV_LORA_RANK, 2 * CONV_CHANNELS, 3 * GDN_W, GDN_W, LANES, LANES)


def _layout_w_in(w_in):
    sizes = (Q_LORA_RANK, KV_LORA_RANK, QK_ROPE_DIM, 2 * CONV_CHANNELS, 3 * GDN_W, GDN_W, 4 * GDN_HEADS)
    w_in = w_in.astype(BF16)
    c_q, c_kv, k_rope, conv, gqkv, gz, gates = jnp.split(w_in, np.cumsum(sizes)[:-1].tolist(), axis=-1)
    misc = _pad_cols(jnp.concatenate([k_rope, gates], axis=-1), LANES)
    krot = _pad_cols(_rot_half_cols(k_rope), LANES)
    return jnp.concatenate([c_q, c_kv, conv, gqkv, gz, misc, krot], axis=-1)


def _layout_w_uq(w_uq):
    lead = w_uq.shape[:-1]
    w = w_uq.reshape(*lead, MLA_HEADS, QK_HEAD_DIM)
    nope = w[..., :QK_NOPE_DIM]
    rope = w[..., QK_NOPE_DIM:]
    parts = [nope, _pad_cols(rope, LANES), _pad_cols(_rot_half_cols(rope), LANES)]
    return jnp.concatenate([p.reshape(*lead, MLA_HEADS * LANES) for p in parts], axis=-1).astype(BF16)


def _layout_w_ukv(w_ukv):
    lead = w_ukv.shape[:-1]
    w = w_ukv.reshape(*lead, MLA_HEADS, QK_NOPE_DIM + V_HEAD_DIM)
    return jnp.concatenate([w[..., :QK_NOPE_DIM].reshape(*lead, -1), w[..., QK_NOPE_DIM:].reshape(*lead, -1)],
                           axis=-1).astype(BF16)


def _layer(x, cos, sin, batch, seq, layer, w_in_stack, w_uq_stack, w_ukv_stack, pre_mix_norm, q_a_norm,
           kv_a_norm, conv_dw_w, conv_dw_b, conv_ln_g, conv_ln_b, gdn_conv_w, gdn_a_log, gdn_dt_bias,
           gdn_out_norm, w_out, post_mix_norm, pre_ffn_norm, w_gate, w_up, w_down, post_ffn_norm):
    cq, ckv, conv_in, gqkv, gz, misc, krot = _in_proj(x, pre_mix_norm, w_in_stack, layer, IN_WIDTHS)
    q_scale = QK_HEAD_DIM ** -0.5 * float(np.log2(np.e))
    q, k_nope, k_rope, v = _mla_prep(cq, ckv, misc, krot, cos, sin, q_a_norm, kv_a_norm,
                                     w_uq_stack, w_ukv_stack, layer, q_scale)
    o_a = _attention(q, k_nope, k_rope, v, batch, seq)
    o_b = _conformer(conv_in, conv_dw_w, conv_dw_b, conv_ln_g, conv_ln_b, seq)
    gq, gk, gv, gates = _gdn_prep(gqkv, misc, gdn_conv_w, gdn_a_log, gdn_dt_bias, seq)
    o_f, o_bw = _gdn(gq, gk, gv, gates, batch, seq)
    x = _out_proj(x, o_a, o_b, o_f, o_bw, gz, gdn_out_norm, w_out, post_mix_norm)
    return _ffn(x, pre_ffn_norm, w_gate, w_up, w_down, post_ffn_norm)


def kernel(x, positions, pre_mix_norm, w_in, q_a_norm, w_uq, kv_a_norm, w_ukv, conv_dw_w, conv_dw_b,
           conv_ln_g, conv_ln_b, gdn_conv_w, gdn_a_log, gdn_dt_bias, gdn_out_norm, w_out, post_mix_norm,
           pre_ffn_norm, w_gate, w_up, w_down, post_ffn_norm):
    batch, seq, d = x.shape
    cos, sin = _rope_tables(positions)
    h = x.reshape(batch * seq, d)
    stacks = (_layout_w_in(w_in), _layout_w_uq(w_uq), _layout_w_ukv(w_ukv))
    per_layer = (pre_mix_norm, q_a_norm, kv_a_norm, conv_dw_w, conv_dw_b, conv_ln_g, conv_ln_b,
                 gdn_conv_w, gdn_a_log, gdn_dt_bias, gdn_out_norm, w_out, post_mix_norm, pre_ffn_norm,
                 w_gate, w_up, w_down, post_ffn_norm)
    big = (w_out, w_gate, w_up, w_down)
    for l in range(pre_mix_norm.shape[0]):
        h = _layer(h, cos, sin, batch, seq, l, *stacks,
                   *(_to_bf16(p, l) if any(p is b for b in big) else p[l] for p in per_layer))
    return h.reshape(batch, seq, d)
```

```python
import functools

import numpy as np
import jax
import jax.numpy as jnp
from jax import lax
from jax.experimental import pallas as pl
from jax.experimental.pallas import tpu as pltpu

F32 = jnp.float32
BF16 = jnp.bfloat16

EPS = 1e-6
LANES = 128
SUBLANES = 8
VMEM_LIMIT = 56 * 1024 * 1024
FFN_VMEM_LIMIT = 62 * 1024 * 1024

MLA_HEADS = 8
Q_LORA_RANK = 512
KV_LORA_RANK = 512
QK_NOPE_DIM = 128
QK_ROPE_DIM = 64
QK_HEAD_DIM = QK_NOPE_DIM + QK_ROPE_DIM
V_HEAD_DIM = 128
ROPE_THETA = 10000.0
QK_PAD_DIM = 2 * LANES
V_PAD_DIM = 2 * LANES
CONV_CHANNELS = 512
CONV_WIDTH = 31
GDN_HEADS = 4
GDN_K_DIM = 128
GDN_V_DIM = 128
GDN_SHORT_CONV = 5
GDN_W = GDN_HEADS * GDN_K_DIM
GDN_CHUNK = 256
GATE_LANE0 = QK_ROPE_DIM

def _params(semantics, vmem_limit=VMEM_LIMIT):
    return pltpu.CompilerParams(dimension_semantics=semantics, vmem_limit_bytes=vmem_limit)


def _resident(shape):
    return pl.BlockSpec(shape, lambda *_: (0,) * len(shape), pipeline_mode=pl.Buffered(1))


def _resident_layer(shape, layer):
    return pl.BlockSpec((None,) + tuple(shape), lambda *_: (layer,) + (0,) * len(shape),
                        pipeline_mode=pl.Buffered(1))


def _rms(x, w):
    return x * lax.rsqrt(jnp.mean(x * x, axis=-1, keepdims=True) + EPS) * w


def _sigmoid(x):
    return 1.0 / (1.0 + jnp.exp(-x))


def _silu(x):
    return x * _sigmoid(x)


def _rope_table_kernel(pos_ref, inv_ref, cos_ref, sin_ref):
    ang = pos_ref[...].astype(F32) * inv_ref[...]
    lane = lax.broadcasted_iota(jnp.int32, ang.shape, 1)
    valid = lane < QK_ROPE_DIM
    cos_ref[...] = jnp.where(valid, jnp.cos(ang), 0.0)
    sin_ref[...] = jnp.where(valid, jnp.sin(ang), 0.0)


def _rope_tables(positions):
    t = positions.size
    tm = min(t, 2048)
    half = QK_ROPE_DIM // 2
    inv = 1.0 / (ROPE_THETA ** (jnp.arange(0, QK_ROPE_DIM, 2, dtype=F32) / QK_ROPE_DIM))
    inv_row = jnp.zeros((1, LANES), F32).at[0, :half].set(inv).at[0, half:2 * half].set(inv)
    spec = pl.BlockSpec((tm, LANES), lambda i: (i, 0))
    return pl.pallas_call(
        _rope_table_kernel,
        grid=(t // tm,),
        in_specs=[pl.BlockSpec((tm, 1), lambda i: (i, 0)), _resident((1, LANES))],
        out_specs=[spec, spec],
        out_shape=[jax.ShapeDtypeStruct((t, LANES), F32)] * 2,
        compiler_params=_params(("parallel",)),
        name="rope_tables",
    )(positions.reshape(t, 1), inv_row)


def _in_proj_kernel(x_ref, g_ref, *refs, groups):
    w_refs, out_refs = refs[:len(groups)], refs[len(groups):]
    half = x_ref.shape[0] // 2
    for r in (0, half):
        rows = slice(r, r + half)
        xn = _rms(x_ref[rows, :], g_ref[...]).astype(BF16)
        outs = iter(out_refs)
        for w_ref, widths in zip(w_refs, groups):
            off = 0
            for n in widths:
                o_ref = next(outs)
                o_ref[rows, :] = jnp.dot(xn, w_ref[:, off:off + n],
                                         preferred_element_type=F32).astype(o_ref.dtype)
                off += n


def _in_proj(x, gain, w_stacks, layer, groups, tm=512):
    t, d = x.shape
    widths = [n for g in groups for n in g]
    assert all(w.shape[2] == sum(g) for w, g in zip(w_stacks, groups))
    return pl.pallas_call(
        functools.partial(_in_proj_kernel, groups=groups),
        grid=(t // tm,),
        in_specs=[pl.BlockSpec((tm, d), lambda i: (i, 0)), _resident((1, d))] + [
            _resident_layer(w.shape[1:], layer) for w in w_stacks],
        out_specs=[pl.BlockSpec((tm, wd), lambda i: (i, 0)) for wd in widths],
        out_shape=[jax.ShapeDtypeStruct((t, wd), F32) for wd in widths],
        compiler_params=_params(("parallel",)),
        name="in_proj",
    )(x, gain.reshape(1, d), *w_stacks)


def _mla_prep_kernel(cq_ref, ckv_ref, misc_ref, krot_ref, cos_ref, sin_ref, qn_ref, kvn_ref,
                     wq_ref, wkv_ref, q_out, kn_out, kr_out, v_out, *, q_scale):
    cos = cos_ref[...]
    sin = sin_ref[...]
    hw = MLA_HEADS * LANES
    cqn = _rms(cq_ref[...], qn_ref[...]).astype(BF16)
    q_nope = jnp.dot(cqn, wq_ref[:, :hw], preferred_element_type=F32)
    q_rope = jnp.dot(cqn, wq_ref[:, hw:2 * hw], preferred_element_type=F32)
    q_rot = jnp.dot(cqn, wq_ref[:, 2 * hw:], preferred_element_type=F32)
    ckvn = _rms(ckv_ref[...], kvn_ref[...]).astype(BF16)
    kn_out[...] = jnp.dot(ckvn, wkv_ref[:, :hw], preferred_element_type=F32).astype(kn_out.dtype)
    v_out[...] = jnp.dot(ckvn, wkv_ref[:, hw:], preferred_element_type=F32).astype(v_out.dtype)
    kr_out[...] = (misc_ref[...] * cos + krot_ref[...] * sin).astype(kr_out.dtype)
    for h in range(MLA_HEADS):
        blk = slice(h * LANES, (h + 1) * LANES)
        lo = slice(h * QK_PAD_DIM, h * QK_PAD_DIM + LANES)
        hi = slice(h * QK_PAD_DIM + LANES, (h + 1) * QK_PAD_DIM)
        q_out[:, lo] = (q_nope[:, blk] * q_scale).astype(q_out.dtype)
        q_out[:, hi] = ((q_rope[:, blk] * cos + q_rot[:, blk] * sin) * q_scale).astype(q_out.dtype)


def _mla_prep(cq, ckv, misc, krot, cos, sin, q_a_norm, kv_a_norm, wq_stack, wkv_stack, layer, q_scale,
              tm=512):
    t = cq.shape[0]
    row = lambda n: pl.BlockSpec((tm, n), lambda i: (i, 0))
    q_w = MLA_HEADS * QK_PAD_DIM
    kn_w = MLA_HEADS * QK_NOPE_DIM
    v_w = MLA_HEADS * V_HEAD_DIM
    bf = lambda n: jax.ShapeDtypeStruct((t, n), BF16)
    return pl.pallas_call(
        functools.partial(_mla_prep_kernel, q_scale=q_scale),
        grid=(t // tm,),
        in_specs=[row(Q_LORA_RANK), row(KV_LORA_RANK), row(LANES), row(LANES), row(LANES), row(LANES),
                  _resident((1, Q_LORA_RANK)), _resident((1, KV_LORA_RANK)),
                  _resident_layer(wq_stack.shape[1:], layer), _resident_layer(wkv_stack.shape[1:], layer)],
        out_specs=[row(q_w), row(kn_w), row(LANES), row(v_w)],
        out_shape=[bf(q_w), bf(kn_w), bf(LANES), bf(v_w)],
        compiler_params=_params(("parallel",)),
        name="mla_prep",
    )(cq, ckv, misc, krot, cos, sin, q_a_norm.reshape(1, -1), kv_a_norm.reshape(1, -1),
      wq_stack, wkv_stack)


ATTN_TK = 512


def _attn_kernel(q_ref, kn_ref, kr_ref, v_ref, o_ref, *, heads):
    tq = q_ref.shape[0]
    m = [jnp.full((tq, 1), -jnp.inf, F32)] * heads
    acc = [jnp.zeros((tq, V_PAD_DIM), F32)] * heads
    ones_col = jnp.where(lax.broadcasted_iota(jnp.int32, (ATTN_TK, LANES), 1) == 0, 1.0, 0.0).astype(BF16)
    for j in range(kn_ref.shape[0] // ATTN_TK):
        rows = slice(j * ATTN_TK, (j + 1) * ATTN_TK)
        k_rope = kr_ref[rows, :]
        for h in range(heads):
            head = slice(h * LANES, (h + 1) * LANES)
            k_op = jnp.concatenate([kn_ref[rows, head], k_rope], axis=1)
            v_op = jnp.concatenate([v_ref[rows, head], ones_col], axis=1)
            s = lax.dot_general(q_ref[:, h * QK_PAD_DIM:(h + 1) * QK_PAD_DIM], k_op,
                                (((1,), (1,)), ((), ())), preferred_element_type=F32)
            m_new = jnp.maximum(m[h], jnp.max(s, axis=-1, keepdims=True))
            p = jnp.exp2(s - m_new).astype(BF16)
            acc[h] = acc[h] * jnp.exp2(m[h] - m_new) + jnp.dot(p, v_op, preferred_element_type=F32)
            m[h] = m_new
    for h in range(heads):
        o_ref[:, h * V_HEAD_DIM:(h + 1) * V_HEAD_DIM] = (
            acc[h][:, :V_HEAD_DIM] / acc[h][:, V_HEAD_DIM:V_HEAD_DIM + 1]).astype(o_ref.dtype)


def _attention(q, k_nope, k_rope, v, batch, seq, tq=1024, heads=4):
    t = q.shape[0]
    nq = seq // tq
    return pl.pallas_call(
        functools.partial(_attn_kernel, heads=heads),
        grid=(batch, MLA_HEADS // heads, nq),
        in_specs=[pl.BlockSpec((tq, heads * QK_PAD_DIM), lambda b, h, i: (b * nq + i, h)),
                  pl.BlockSpec((seq, heads * QK_NOPE_DIM), lambda b, h, i: (b, h)),
                  pl.BlockSpec((seq, LANES), lambda b, h, i: (b, 0)),
                  pl.BlockSpec((seq, heads * V_HEAD_DIM), lambda b, h, i: (b, h))],
        out_specs=pl.BlockSpec((tq, heads * V_HEAD_DIM), lambda b, h, i: (b * nq + i, h)),
        out_shape=jax.ShapeDtypeStruct((t, MLA_HEADS * V_HEAD_DIM), BF16),
        compiler_params=_params(("parallel", "parallel", "arbitrary")),
        name="mla_attention",
    )(q, k_nope, k_rope, v)


def _halo_specs(tm, halo, width, n_rows):
    per = tm // halo
    last = n_rows // halo - 1
    return [pl.BlockSpec((tm, width), lambda i: (i, 0)),
            pl.BlockSpec((halo, width), lambda i: (jnp.maximum(i * per - 1, 0), 0)),
            pl.BlockSpec((halo, width), lambda i: (jnp.minimum((i + 1) * per, last), 0))]


def _seq_edges(tiles_per_seq):
    i = pl.program_id(0) % tiles_per_seq
    return i == 0, i == tiles_per_seq - 1


CONF_HALO = 16
CONF_ROWS = 128


def _conformer_kernel(cur_ref, prev_ref, next_ref, w_ref, b_ref, g_ref, beta_ref, o_ref, ext_ref,
                      shift_ref, *, tiles_per_seq):
    tm = cur_ref.shape[0]
    c = CONV_CHANNELS
    first, last = _seq_edges(tiles_per_seq)

    def glu(u):
        return u[:, :c] * _sigmoid(u[:, c:])

    ext_ref[0:CONF_HALO, :] = jnp.where(first, 0.0, glu(prev_ref[...]))
    ext_ref[CONF_HALO:CONF_HALO + tm, :] = glu(cur_ref[...])
    ext_ref[CONF_HALO + tm:, :] = jnp.where(last, 0.0, glu(next_ref[...]))
    n_shift = shift_ref.shape[1]
    for s in range(1, SUBLANES):
        shift_ref[s, :, :] = ext_ref[s:s + n_shift, :]
    pad = CONV_WIDTH // 2
    bias = b_ref[...]
    for r in range(0, tm, CONF_ROWS):
        acc = jnp.broadcast_to(bias, (CONF_ROWS, c))
        for d in range(CONV_WIDTH):
            start = CONF_HALO + r - pad + d
            s = start % SUBLANES
            base = start - s
            win = ext_ref[base:base + CONF_ROWS, :] if s == 0 else shift_ref[s, base:base + CONF_ROWS, :]
            acc = acc + win * w_ref[d:d + 1, :]
        mu = jnp.mean(acc, axis=-1, keepdims=True)
        cen = acc - mu
        var = jnp.mean(cen * cen, axis=-1, keepdims=True)
        y = cen * lax.rsqrt(var + EPS) * g_ref[...] + beta_ref[...]
        o_ref[r:r + CONF_ROWS, :] = _silu(y).astype(o_ref.dtype)


def _conformer(conv_in, dw_w, dw_b, ln_g, ln_b, seq, tm=512):
    t = conv_in.shape[0]
    c = CONV_CHANNELS
    return pl.pallas_call(
        functools.partial(_conformer_kernel, tiles_per_seq=seq // tm),
        grid=(t // tm,),
        in_specs=_halo_specs(tm, CONF_HALO, 2 * c, t) + [
            _resident((CONV_WIDTH, c)), _resident((1, c)), _resident((1, c)), _resident((1, c))],
        out_specs=pl.BlockSpec((tm, c), lambda i: (i, 0)),
        out_shape=jax.ShapeDtypeStruct((t, c), BF16),
        scratch_shapes=[pltpu.VMEM((tm + 2 * CONF_HALO, c), F32),
                        pltpu.VMEM((SUBLANES, tm + 2 * CONF_HALO - SUBLANES, c), F32)],
        compiler_params=_params(("parallel",)),
        name="conformer_conv",
    )(conv_in, conv_in, conv_in, dw_w.reshape(CONV_WIDTH, c), dw_b.reshape(1, c),
      ln_g.reshape(1, c), ln_b.reshape(1, c))


GDN_HALO = 8
GDN_ROWS = 64


def _gdn_prep_kernel(cur_ref, prev_ref, next_ref, misc_ref, w_ref, alog_ref, dtb_ref,
                     q_out, k_out, v_out, gate_out, ext_ref, *, tiles_per_seq):
    tm = cur_ref.shape[0]
    first, last = _seq_edges(tiles_per_seq)
    ext_ref[0:GDN_HALO, :] = jnp.where(first, 0.0, prev_ref[...])
    ext_ref[GDN_HALO:GDN_HALO + tm, :] = cur_ref[...]
    ext_ref[GDN_HALO + tm:, :] = jnp.where(last, 0.0, next_ref[...])
    pad = GDN_SHORT_CONV // 2
    for r in range(0, tm, GDN_ROWS):
        rows = slice(r, r + GDN_ROWS)
        for part, out in enumerate((q_out, k_out, v_out)):
            cols = slice(part * GDN_W, (part + 1) * GDN_W)
            acc = jnp.zeros((GDN_ROWS, GDN_W), F32)
            for d in range(GDN_SHORT_CONV):
                start = GDN_HALO + r - pad + d
                acc = acc + ext_ref[start:start + GDN_ROWS, cols] * w_ref[d:d + 1, cols]
            y = _silu(acc)
            if part == 2:
                out[rows, :] = y
                continue
            scale = GDN_K_DIM ** -0.5 if part == 0 else 1.0
            for h in range(GDN_HEADS):
                blk = slice(h * GDN_K_DIM, (h + 1) * GDN_K_DIM)
                yh = y[:, blk]
                inv = lax.rsqrt(jnp.sum(yh * yh, axis=-1, keepdims=True) + EPS)
                out[rows, blk] = yh * (inv * scale)
    m = misc_ref[...]
    lane = lax.broadcasted_iota(jnp.int32, m.shape, 1) - GATE_LANE0
    is_gate = (lane >= 0) & (lane < 4 * GDN_HEADS)
    is_a = is_gate & ((lane // GDN_HEADS) % 2 == 0)
    z = m + dtb_ref[...]
    softplus = jnp.maximum(z, 0.0) + jnp.log(1.0 + jnp.exp(-jnp.abs(z)))
    g = -jnp.exp(alog_ref[...]) * softplus
    gate_out[...] = jnp.where(is_a, g, jnp.where(is_gate, _sigmoid(m), 0.0))


def _gdn_prep(gqkv, misc, conv_w, a_log, dt_bias, seq, tm=512):
    t = gqkv.shape[0]
    w3 = 3 * GDN_W
    zeros = jnp.zeros((GDN_HEADS,), F32)
    lane_vals = lambda p: jnp.zeros((1, LANES), F32).at[0, GATE_LANE0:GATE_LANE0 + 4 * GDN_HEADS].set(
        jnp.concatenate([p[0], zeros, p[1], zeros]))
    row = lambda n: pl.BlockSpec((tm, n), lambda i: (i, 0))
    return pl.pallas_call(
        functools.partial(_gdn_prep_kernel, tiles_per_seq=seq // tm),
        grid=(t // tm,),
        in_specs=_halo_specs(tm, GDN_HALO, w3, t) + [
            row(LANES), _resident((GDN_SHORT_CONV, w3)), _resident((1, LANES)), _resident((1, LANES))],
        out_specs=[row(GDN_W), row(GDN_W), row(GDN_W), row(LANES)],
        out_shape=[jax.ShapeDtypeStruct((t, GDN_W), F32)] * 3 + [jax.ShapeDtypeStruct((t, LANES), F32)],
        scratch_shapes=[pltpu.VMEM((tm + 2 * GDN_HALO, w3), F32)],
        compiler_params=_params(("parallel",)),
        name="gdn_prep",
    )(gqkv, gqkv, gqkv, misc, conv_w.reshape(GDN_SHORT_CONV, w3), lane_vals(a_log), lane_vals(dt_bias))


def _bdot(a, b):
    return jnp.dot(a.astype(BF16), b.astype(BF16), preferred_element_type=F32)


def _bdot_nt(a, b):
    return lax.dot_general(a.astype(BF16), b.astype(BF16), (((1,), (1,)), ((), ())),
                           preferred_element_type=F32)


def _gdn_kernel(qf_ref, kf_ref, vf_ref, gf_ref, qb_ref, kb_ref, vb_ref, gb_ref, of_ref, ob_ref,
                state_ref, *, heads):
    c = qf_ref.shape[0]
    d = GDN_K_DIM

    @pl.when(pl.program_id(2) == 0)
    def _():
        state_ref[...] = jnp.zeros_like(state_ref)

    ri = lax.broadcasted_iota(jnp.int32, (c, c), 0)
    ci = lax.broadcasted_iota(jnp.int32, (c, c), 1)
    xor = ri ^ ci
    lane_row = lax.broadcasted_iota(jnp.int32, (1, LANES), 1)
    lane_col = lax.broadcasted_iota(jnp.int32, (LANES, 1), 0)

    chains = []
    for direction, (q_ref, k_ref, v_ref, g_ref, o_ref) in enumerate(
            ((qf_ref, kf_ref, vf_ref, gf_ref, of_ref), (qb_ref, kb_ref, vb_ref, gb_ref, ob_ref))):
        lower = direction == 0
        incl = (ri >= ci) if lower else (ri <= ci)
        strict = (ri > ci) if lower else (ri < ci)
        gates = g_ref[...]
        gc_all = jnp.dot(jnp.where(incl, 1.0, 0.0), gates, preferred_element_type=F32,
                         precision=lax.Precision.HIGHEST)
        gc_all_t = gc_all.T
        last = c - 1 if lower else 0
        for h in range(heads):
            blk = slice(h * d, (h + 1) * d)
            g_lane = GATE_LANE0 + 2 * GDN_HEADS * direction + pl.program_id(1) * heads + h
            gc = jnp.sum(jnp.where(lane_row == g_lane, gc_all, 0.0), axis=1, keepdims=True)
            beta = jnp.sum(jnp.where(lane_row == g_lane + GDN_HEADS, gates, 0.0), axis=1, keepdims=True)
            gc_row = jnp.sum(jnp.where(lane_col == g_lane, gc_all_t, 0.0), axis=0, keepdims=True)
            chains.append(dict(q=q_ref[:, blk], k=k_ref[:, blk], v=v_ref[:, blk], gc=gc, beta=beta,
                               gc_row=gc_row, g_tot=gc[last:last + 1, :], incl=incl, strict=strict,
                               o_ref=o_ref, blk=blk, idx=(direction, h)))

    for ch in chains:
        ch["kk"] = _bdot_nt(ch["k"], ch["k"])
    for ch in chains:
        ch["qk"] = _bdot_nt(ch["q"], ch["k"])
    for ch in chains:
        incl = ch["incl"]
        decay = jnp.where(incl, jnp.exp(jnp.where(incl, ch["gc"] - ch["gc_row"], 0.0)), 0.0)
        ch["a"] = jnp.where(ch["strict"], ch["kk"] * ch["beta"] * decay, 0.0)
        ch["qkd"] = (ch["qk"] * decay).astype(BF16)
        ch["t"] = jnp.where(ri == ci, 1.0, 0.0) - jnp.where(xor < 2, ch["a"], 0.0)
    b = 2
    while b < c:
        level = (xor >= b) & (xor < 2 * b)
        for ch in chains:
            ch["x"] = _bdot(jnp.where(level, ch["a"], 0.0), ch["t"])
        for ch in chains:
            ch["t"] = ch["t"] - _bdot(ch["t"], ch["x"])
        b *= 2
    for ch in chains:
        rhs = jnp.concatenate([ch["v"] * ch["beta"], ch["k"] * (ch["beta"] * jnp.exp(ch["gc"]))], axis=1)
        ch["uw"] = _bdot(ch["t"], rhs).astype(BF16)
    for ch in chains:
        ch["z"] = _bdot(ch["qkd"], ch["uw"])
    for ch in chains:
        k_til = ch["k"] * jnp.exp(ch["g_tot"] - ch["gc"])
        ch["y"] = _bdot(k_til.T, ch["uw"])
    for ch in chains:
        state = state_ref[ch["idx"]]
        q_eff = ch["q"] * jnp.exp(ch["gc"]) - ch["z"][:, d:]
        ch["o_ref"][:, ch["blk"]] = _bdot(q_eff, state) + ch["z"][:, :d]
        state_ref[ch["idx"]] = (state * jnp.exp(ch["g_tot"]) - _bdot(ch["y"][:, d:], state)
                                + ch["y"][:, :d])


def _gdn(q, k, v, gates, batch, seq, heads=GDN_HEADS):
    t = q.shape[0]
    c = GDN_CHUNK
    nc = seq // c
    hw = heads * GDN_K_DIM
    gate_f = pl.BlockSpec((c, LANES), lambda b, hp, i: (b * nc + i, 0))
    gate_b = pl.BlockSpec((c, LANES), lambda b, hp, i: (b * nc + nc - 1 - i, 0))
    head_f = pl.BlockSpec((c, hw), lambda b, hp, i: (b * nc + i, hp))
    head_b = pl.BlockSpec((c, hw), lambda b, hp, i: (b * nc + nc - 1 - i, hp))
    return pl.pallas_call(
        functools.partial(_gdn_kernel, heads=heads),
        grid=(batch, GDN_HEADS // heads, nc),
        in_specs=[head_f, head_f, head_f, gate_f, head_b, head_b, head_b, gate_b],
        out_specs=[head_f, head_b],
        out_shape=[jax.ShapeDtypeStruct((t, GDN_W), F32)] * 2,
        scratch_shapes=[pltpu.VMEM((2, heads, GDN_K_DIM, GDN_V_DIM), F32)],
        compiler_params=_params(("parallel", "parallel", "arbitrary")),
        name="gdn_delta_rule",
    )(q, k, v, gates, q, k, v, gates)


def _out_proj_kernel(x_ref, oa_ref, ob_ref, of_ref, obw_ref, z_ref, onorm_ref, w_ref, g_ref, o_ref):
    na = oa_ref.shape[1]
    nb = ob_ref.shape[1]
    half = x_ref.shape[0] // 2
    for r in (0, half):
        rows = slice(r, r + half)
        oc = of_ref[rows, :] + obw_ref[rows, :]
        z = z_ref[rows, :]
        parts = []
        for h in range(GDN_HEADS):
            blk = slice(h * GDN_V_DIM, (h + 1) * GDN_V_DIM)
            parts.append(_rms(oc[:, blk], onorm_ref[...]) * _silu(z[:, blk]))
        og = jnp.concatenate(parts, axis=1).astype(BF16)
        mix = (jnp.dot(oa_ref[rows, :], w_ref[:na, :], preferred_element_type=F32)
               + jnp.dot(ob_ref[rows, :], w_ref[na:na + nb, :], preferred_element_type=F32)
               + jnp.dot(og, w_ref[na + nb:, :], preferred_element_type=F32))
        o_ref[rows, :] = x_ref[rows, :] + _rms(mix, g_ref[...])


def _out_proj(x, o_a, o_b, o_f, o_bw, z, out_norm, w_out, gain, tm=512):
    t, d = x.shape
    row = lambda n: pl.BlockSpec((tm, n), lambda i: (i, 0))
    return pl.pallas_call(
        _out_proj_kernel,
        grid=(t // tm,),
        in_specs=[row(d), row(o_a.shape[1]), row(o_b.shape[1]), row(GDN_W), row(GDN_W), row(GDN_W),
                  _resident((1, GDN_V_DIM)), _resident(w_out.shape), _resident((1, d))],
        out_specs=row(d),
        out_shape=jax.ShapeDtypeStruct((t, d), F32),
        compiler_params=_params(("parallel",)),
        name="out_proj",
    )(x, o_a, o_b, o_f, o_bw, z, out_norm.reshape(1, -1), w_out, gain.reshape(1, d))


def _ffn_kernel(x_ref, gin_ref, wg_ref, wu_ref, wd_ref, gout_ref, o_ref, hn_ref):
    j = pl.program_id(1)

    @pl.when(j == 0)
    def _():
        hn_ref[...] = _rms(x_ref[...], gin_ref[...]).astype(BF16)
        o_ref[...] = jnp.zeros_like(o_ref)

    hn = hn_ref[...]
    gate = jnp.dot(hn, wg_ref[...], preferred_element_type=F32)
    up = jnp.dot(hn, wu_ref[...], preferred_element_type=F32)
    o_ref[...] += jnp.dot((_silu(gate) * up).astype(BF16), wd_ref[...], preferred_element_type=F32)

    @pl.when(j == pl.num_programs(1) - 1)
    def _():
        o_ref[...] = x_ref[...] + _rms(o_ref[...], gout_ref[...])


def _ffn(x, gain_in, w_gate, w_up, w_down, gain_out, tm=1024, tf=512):
    t, d = x.shape
    f = w_gate.shape[1]
    tm = min(tm, t)
    return pl.pallas_call(
        _ffn_kernel,
        grid=(t // tm, f // tf),
        in_specs=[pl.BlockSpec((tm, d), lambda i, j: (i, 0)),
                  _resident((1, d)),
                  pl.BlockSpec((d, tf), lambda i, j: (0, j)),
                  pl.BlockSpec((d, tf), lambda i, j: (0, j)),
                  pl.BlockSpec((tf, d), lambda i, j: (j, 0)),
                  _resident((1, d))],
        out_specs=pl.BlockSpec((tm, d), lambda i, j: (i, 0)),
        out_shape=jax.ShapeDtypeStruct((t, d), F32),
        scratch_shapes=[pltpu.VMEM((tm, d), BF16)],
        compiler_params=_params(("parallel", "arbitrary"), FFN_VMEM_LIMIT),
        name="swiglu_ffn",
    )(x, gain_in.reshape(1, d), w_gate, w_up, w_down, gain_out.reshape(1, d))


def _cast_kernel(x_ref, o_ref):
    o_ref[...] = x_ref[...].astype(o_ref.dtype)


def _to_bf16(w_stack, layer, rows=256):
    _, r, c = w_stack.shape
    return pl.pallas_call(
        _cast_kernel, grid=(r // rows,),
        in_specs=[pl.BlockSpec((None, rows, c), lambda i: (layer, i, 0))],
        out_specs=pl.BlockSpec((rows, c), lambda i: (i, 0)),
        out_shape=jax.ShapeDtypeStruct((r, c), BF16),
        compiler_params=_params(("parallel",)), name="to_bf16",
    )(w_stack)


def _rot_half_cols(w):
    half = w.shape[-1] // 2
    return jnp.concatenate([-w[..., half:], w[..., :half]], axis=-1)


def _pad_cols(w, n):
    return jnp.pad(w, [(0, 0)] * (w.ndim - 1) + [(0, n - w.shape[-1])])


IN_GROUPS = ((Q_LORA_RANK, KV_LORA_RANK), (2 * CONV_CHANNELS, 3 * GDN_W, GDN_W), (LANES, LANES))


def _layout_w_in(w_in):
    w_in = w_in.astype(BF16)
    head_w = Q_LORA_RANK + KV_LORA_RANK
    body_lo = head_w + QK_ROPE_DIM
    body_hi = body_lo + sum(IN_GROUPS[1])
    k_rope = w_in[..., head_w:body_lo]
    misc = _pad_cols(jnp.concatenate([k_rope, w_in[..., body_hi:]], axis=-1), LANES)
    krot = _pad_cols(_rot_half_cols(k_rope), LANES)
    return w_in[..., :head_w], w_in[..., body_lo:body_hi], jnp.concatenate([misc, krot], axis=-1)


def _layout_w_uq(w_uq):
    lead = w_uq.shape[:-1]
    w = w_uq.reshape(*lead, MLA_HEADS, QK_HEAD_DIM)
    nope = w[..., :QK_NOPE_DIM]
    rope = w[..., QK_NOPE_DIM:]
    parts = [nope, _pad_cols(rope, LANES), _pad_cols(_rot_half_cols(rope), LANES)]
    return jnp.concatenate([p.reshape(*lead, MLA_HEADS * LANES) for p in parts], axis=-1).astype(BF16)


def _layout_w_ukv(w_ukv):
    lead = w_ukv.shape[:-1]
    w = w_ukv.reshape(*lead, MLA_HEADS, QK_NOPE_DIM + V_HEAD_DIM)
    return jnp.concatenate([w[..., :QK_NOPE_DIM].reshape(*lead, -1), w[..., QK_NOPE_DIM:].reshape(*lead, -1)],
                           axis=-1).astype(BF16)


def _layer(x, cos, sin, batch, seq, layer, w_in_stacks, w_uq_stack, w_ukv_stack, pre_mix_norm, q_a_norm,
           kv_a_norm, conv_dw_w, conv_dw_b, conv_ln_g, conv_ln_b, gdn_conv_w, gdn_a_log, gdn_dt_bias,
           gdn_out_norm, w_out, post_mix_norm, pre_ffn_norm, w_gate, w_up, w_down, post_ffn_norm):
    cq, ckv, conv_in, gqkv, gz, misc, krot = _in_proj(x, pre_mix_norm, w_in_stacks, layer, IN_GROUPS)
    q_scale = QK_HEAD_DIM ** -0.5 * float(np.log2(np.e))
    q, k_nope, k_rope, v = _mla_prep(cq, ckv, misc, krot, cos, sin, q_a_norm, kv_a_norm,
                                     w_uq_stack, w_ukv_stack, layer, q_scale)
    o_a = _attention(q, k_nope, k_rope, v, batch, seq)
    o_b = _conformer(conv_in, conv_dw_w, conv_dw_b, conv_ln_g, conv_ln_b, seq)
    gq, gk, gv, gates = _gdn_prep(gqkv, misc, gdn_conv_w, gdn_a_log, gdn_dt_bias, seq)
    o_f, o_bw = _gdn(gq, gk, gv, gates, batch, seq)
    x = _out_proj(x, o_a, o_b, o_f, o_bw, gz, gdn_out_norm, w_out, post_mix_norm)
    return _ffn(x, pre_ffn_norm, w_gate, w_up, w_down, post_ffn_norm)


def kernel(x, positions, pre_mix_norm, w_in, q_a_norm, w_uq, kv_a_norm, w_ukv, conv_dw_w, conv_dw_b,
           conv_ln_g, conv_ln_b, gdn_conv_w, gdn_a_log, gdn_dt_bias, gdn_out_norm, w_out, post_mix_norm,
           pre_ffn_norm, w_gate, w_up, w_down, post_ffn_norm):
    batch, seq, d = x.shape
    cos, sin = _rope_tables(positions)
    h = x.reshape(batch * seq, d)
    stacks = (_layout_w_in(w_in), _layout_w_uq(w_uq), _layout_w_ukv(w_ukv))
    per_layer = (pre_mix_norm, q_a_norm, kv_a_norm, conv_dw_w, conv_dw_b, conv_ln_g, conv_ln_b,
                 gdn_conv_w, gdn_a_log, gdn_dt_bias, gdn_out_norm, w_out, post_mix_norm, pre_ffn_norm,
                 w_gate, w_up, w_down, post_ffn_norm)
    big = (w_out, w_gate, w_up, w_down)
    for l in range(pre_mix_norm.shape[0]):
        h = _layer(h, cos, sin, batch, seq, l, *stacks,
                   *(_to_bf16(p, l) if any(p is b for b in big) else p[l] for p in per_layer))
    return h.reshape(batch, seq, d)
```
